```python
import math
import jax, jax.numpy as jnp
from jax import lax
import numpy as np

D_MODEL = 1024
BATCH = 4
SEQ = 8192
DEPTH = 1

MIX_WIDTH = D_MODEL
GLA_WIDTH = MIX_WIDTH // 2
GMLP_WIDTH = MIX_WIDTH - GLA_WIDTH
GLA_HEADS = 4
GLA_DV = GLA_WIDTH // GLA_HEADS
GLA_DK = GLA_DV // 2
GLA_KEY_WIDTH = GLA_HEADS * GLA_DK
GLA_LOWRANK = 16
GLA_TAU = 16.0
GLA_CHUNK = 64
GMLP_GROUPS = 4
GMLP_GROUP_DIM = GMLP_WIDTH // GMLP_GROUPS
GMLP_CHUNK = 128
D_FF = int(math.ceil(8 * D_MODEL / 3 / 256) * 256)
EPS = 1e-6

PROJ_SIZES = [GLA_KEY_WIDTH, GLA_KEY_WIDTH, GLA_WIDTH, GLA_WIDTH,
              GLA_LOWRANK, GLA_LOWRANK, 2 * GMLP_WIDTH]
PROJ_WIDTH = sum(PROJ_SIZES)
PROJ_SPLITS = [int(v) for v in np.cumsum(PROJ_SIZES)[:-1]]

kernel_name = "hybrid_gla_gmlp_encoder_block"


def rmsnorm(x, g):
    xf = x.astype(jnp.float32)
    y = xf * lax.rsqrt(jnp.mean(xf * xf, axis=-1, keepdims=True) + EPS)
    return (y * g.astype(jnp.float32)).astype(x.dtype)


def layernorm(x, g, b):
    xf = x.astype(jnp.float32)
    mu = jnp.mean(xf, axis=-1, keepdims=True)
    xc = xf - mu
    y = xc * lax.rsqrt(jnp.mean(xc * xc, axis=-1, keepdims=True) + EPS)
    return (y * g.astype(jnp.float32) + b.astype(jnp.float32)).astype(x.dtype)


def gla_one_direction(q, k, v, log_a):
    B, S, H, DK = q.shape
    DV = v.shape[-1]
    C = GLA_CHUNK
    N = S // C
    f32 = jnp.float32
    q = q.astype(f32).reshape(B, N, C, H, DK)
    k = k.astype(f32).reshape(B, N, C, H, DK)
    v = v.astype(f32).reshape(B, N, C, H, DV)
    b = jnp.cumsum(log_a.astype(f32).reshape(B, N, C, H, DK), axis=2)
    b_last = b[:, :, -1]
    q_dec = q * jnp.exp(b)
    k_dec = k * jnp.exp(-b)
    k_to_end = k * jnp.exp(b_last[:, :, None] - b)
    scores = jnp.einsum('bnthd,bnshd->bnhts', q_dec, k_dec)
    tril = jnp.tril(jnp.ones((C, C), dtype=bool))
    scores = jnp.where(tril, scores, 0.0)
    o_intra = jnp.einsum('bnhts,bnshv->bnthv', scores, v)
    d_state = jnp.einsum('bnshd,bnshv->bnhdv', k_to_end, v)
    chunk_decay = jnp.exp(b_last)

    def step(state, inp):
        ds, dec = inp
        return dec[..., None] * state + ds, state

    state0 = jnp.zeros((B, H, DK, DV), f32)
    _, states_before = lax.scan(step, state0,
                                (jnp.moveaxis(d_state, 1, 0), jnp.moveaxis(chunk_decay, 1, 0)))
    states_before = jnp.moveaxis(states_before, 0, 1)
    o_inter = jnp.einsum('bnthd,bnhdv->bnthv', q_dec, states_before)
    return (o_intra + o_inter).reshape(B, S, H, DV)


def gla_mixer(h_q, h_k, h_v, h_g, lr_f, lr_b, w_decay_f, b_decay_f, w_decay_b, b_decay_b, gla_norm_g):
    B, S, _ = h_q.shape
    f32 = jnp.float32
    q = h_q.reshape(B, S, GLA_HEADS, GLA_DK) * (GLA_DK ** -0.5)
    k = h_k.reshape(B, S, GLA_HEADS, GLA_DK)
    v = h_v.reshape(B, S, GLA_HEADS, GLA_DV)
    la_f = (jax.nn.log_sigmoid((lr_f @ w_decay_f + b_decay_f).astype(f32)) / GLA_TAU
            ).reshape(B, S, GLA_HEADS, GLA_DK)
    la_b = (jax.nn.log_sigmoid((lr_b @ w_decay_b + b_decay_b).astype(f32)) / GLA_TAU
            ).reshape(B, S, GLA_HEADS, GLA_DK)
    o_fwd = gla_one_direction(q, k, v, la_f)
    o_bwd = jnp.flip(gla_one_direction(jnp.flip(q, 1), jnp.flip(k, 1), jnp.flip(v, 1),
                                       jnp.flip(la_b, 1)), 1)
    o = o_fwd + o_bwd
    o = o * lax.rsqrt(jnp.mean(o * o, axis=-1, keepdims=True) + EPS)
    o = o.reshape(B, S, GLA_WIDTH) * gla_norm_g.astype(f32)
    return (o * jax.nn.silu(h_g.astype(f32))).astype(h_q.dtype)


def gmlp_mixer(h_uv, ln_g, ln_b, w_spatial, b_spatial):
    B, S, _ = h_uv.shape
    z = jax.nn.gelu(h_uv, approximate=False)
    u, v = jnp.split(z, 2, axis=-1)
    v = layernorm(v, ln_g, ln_b)
    v = v.reshape(B, S // GMLP_CHUNK, GMLP_CHUNK, GMLP_GROUPS, GMLP_GROUP_DIM)
    s = jnp.einsum('gij,bnjgc->bnigc', w_spatial, v) + b_spatial.T[None, None, :, :, None]
    return u * s.reshape(B, S, GMLP_WIDTH)


def setup_inputs(seed: int = 0) -> dict:
    key = jax.random.key(seed)
    ks = jax.random.split(key, 20)
    L = DEPTH
    nrm = lambda k, shape, fan_in: jax.random.normal(k, shape, jnp.float32) * (fan_in ** -0.5)
    gain = lambda k, shape: 1.0 + 0.02 * jax.random.normal(k, shape, jnp.float32)
    small = lambda k, shape: 0.01 * jax.random.normal(k, shape, jnp.float32)
    return {
        "x": jax.random.normal(ks[0], (BATCH, SEQ, D_MODEL), jnp.float32),
        "norm1_g": gain(ks[1], (L, D_MODEL)),
        "w_in": nrm(ks[2], (L, D_MODEL, PROJ_WIDTH), D_MODEL),
        "w_decay_f": nrm(ks[3], (L, GLA_LOWRANK, GLA_KEY_WIDTH), GLA_LOWRANK),
        "b_decay_f": small(ks[4], (L, GLA_KEY_WIDTH)),
        "w_decay_b": nrm(ks[5], (L, GLA_LOWRANK, GLA_KEY_WIDTH), GLA_LOWRANK),
        "b_decay_b": small(ks[6], (L, GLA_KEY_WIDTH)),
        "gla_norm_g": gain(ks[7], (L, GLA_WIDTH)),
        "gmlp_ln_g": gain(ks[8], (L, GMLP_WIDTH)),
        "gmlp_ln_b": small(ks[9], (L, GMLP_WIDTH)),
        "w_spatial": nrm(ks[10], (L, GMLP_GROUPS, GMLP_CHUNK, GMLP_CHUNK), GMLP_CHUNK),
        "b_spatial": gain(ks[11], (L, GMLP_GROUPS, GMLP_CHUNK)),
        "w_out": nrm(ks[12], (L, MIX_WIDTH, D_MODEL), MIX_WIDTH),
        "norm2_g": gain(ks[13], (L, D_MODEL)),
        "w_gate": nrm(ks[14], (L, D_MODEL, D_FF), D_MODEL),
        "w_up": nrm(ks[15], (L, D_MODEL, D_FF), D_MODEL),
        "w_down": nrm(ks[16], (L, D_FF, D_MODEL), D_FF),
        "final_norm_g": gain(ks[17], (D_MODEL,)),
    }


def reference(x, norm1_g, w_in, w_decay_f, b_decay_f, w_decay_b, b_decay_b, gla_norm_g,
              gmlp_ln_g, gmlp_ln_b, w_spatial, b_spatial, w_out, norm2_g, w_gate, w_up,
              w_down, final_norm_g):
    for l in range(DEPTH):
        h = rmsnorm(x, norm1_g[l])
        p = h @ w_in[l]
        h_q, h_k, h_v, h_g, lr_f, lr_b, h_uv = jnp.split(p, PROJ_SPLITS, axis=-1)
        y_a = gla_mixer(h_q, h_k, h_v, h_g, lr_f, lr_b, w_decay_f[l], b_decay_f[l],
                        w_decay_b[l], b_decay_b[l], gla_norm_g[l])
        y_b = gmlp_mixer(h_uv, gmlp_ln_g[l], gmlp_ln_b[l], w_spatial[l], b_spatial[l])
        x = x + jnp.concatenate([y_a, y_b.astype(y_a.dtype)], axis=-1) @ w_out[l]
        h2 = rmsnorm(x, norm2_g[l])
        x = x + (jax.nn.silu(h2 @ w_gate[l]) * (h2 @ w_up[l])) @ w_down[l]
    return rmsnorm(x, final_norm_g)
```

```python
import functools

import jax
import jax.numpy as jnp
from jax import lax
from jax.experimental import pallas as pl
from jax.experimental.pallas import tpu as pltpu

F32 = jnp.float32
BF16 = jnp.bfloat16

D_MODEL = 1024
GLA_HEADS = 4
GLA_DK = 64
GLA_DV = 128
GLA_KEY_WIDTH = GLA_HEADS * GLA_DK
GLA_WIDTH = GLA_HEADS * GLA_DV
GLA_LOWRANK = 16
GLA_TAU = 16.0
GLA_CHUNK = 64
GMLP_GROUPS = 4
GMLP_GROUP_DIM = 128
GMLP_WIDTH = GMLP_GROUPS * GMLP_GROUP_DIM
GMLP_CHUNK = 128
D_FF = 2816
EPS = 1e-6

LANES = 128
COL_Q = 0
COL_K = COL_Q + GLA_KEY_WIDTH
COL_V = COL_K + GLA_KEY_WIDTH
COL_G = COL_V + GLA_WIDTH
COL_UV = COL_G + GLA_WIDTH
COL_LR = COL_UV + 2 * GMLP_WIDTH
PROJ_PAD = COL_LR + LANES
QKVG_WIDTH = COL_UV

SEQ_TILE = 512
FFN_TILE = 512
VMEM_LIMIT_BYTES = 56 * 1024 * 1024

_NT = (((1,), (1,)), ((), ()))
_TN = (((0,), (0,)), ((), ()))


def _dot(a, b):
    return jnp.dot(a, b, preferred_element_type=F32)


def _rms(x, g):
    return x * lax.rsqrt(jnp.mean(x * x, axis=-1, keepdims=True) + EPS) * g


def _log_sigmoid(z):
    return jnp.minimum(z, 0.0) - jnp.log1p(jnp.exp(-jnp.abs(z)))


def _chunk_log_decay(la, reverse):
    r = lax.broadcasted_iota(jnp.int32, (GLA_CHUNK, GLA_CHUNK), 0)
    c = lax.broadcasted_iota(jnp.int32, (GLA_CHUNK, GLA_CHUNK), 1)
    tri = jnp.where((c >= r) if reverse else (c <= r), 1.0, 0.0).astype(BF16)
    hi = la.astype(BF16)
    lo = (la - hi.astype(F32)).astype(BF16)
    return _dot(tri, hi) + _dot(tri, lo)


def _gla_tile(q, k, v, la, state_ref, b_scr, o_write, reverse):
    n_tok = q.shape[0]
    n_chunks = n_tok // GLA_CHUNK
    for c in range(n_chunks):
        rows = slice(c * GLA_CHUNK, (c + 1) * GLA_CHUNK)
        b_c = _chunk_log_decay(la[rows, :], reverse)
        for p in range(GLA_HEADS // 2):
            b_scr[p, rows, :] = b_c[:, p * LANES:(p + 1) * LANES]

    last = 0 if reverse else GLA_CHUNK - 1
    pad = jnp.zeros((LANES - n_chunks, LANES), F32)
    dec_t = []
    for p in range(GLA_HEADS // 2):
        b_last = b_scr[p, pl.ds(last, n_chunks, stride=GLA_CHUNK), :]
        dec_t.append(jnp.exp(jnp.concatenate([b_last, pad], axis=0).T))

    ri = lax.broadcasted_iota(jnp.int32, (GLA_CHUNK, LANES), 0)
    ci = lax.broadcasted_iota(jnp.int32, (GLA_CHUNK, LANES), 1)
    first_head = ci < GLA_DK
    ti = ci & (GLA_DK - 1)
    causal = (ti >= ri) if reverse else (ti <= ri)
    zeros_v = jnp.zeros((GLA_CHUNK, GLA_DV), BF16)

    order = range(n_chunks - 1, -1, -1) if reverse else range(n_chunks)
    for c in order:
        rows = slice(c * GLA_CHUNK, (c + 1) * GLA_CHUNK)
        b = jnp.concatenate([b_scr[p, rows, :] for p in range(GLA_HEADS // 2)], axis=1)
        e_pos = jnp.exp(b)
        e_neg = jnp.exp(-b)
        e_last = jnp.exp(b[last:last + 1, :])
        qd = (q[rows, :] * (e_pos * GLA_DK ** -0.5)).astype(BF16)
        kd = k[rows, :] * e_neg
        kte = (kd * e_last).astype(BF16)
        vb = v[rows, :].astype(BF16)
        for p in range(GLA_HEADS // 2):
            lanes = slice(p * LANES, (p + 1) * LANES)
            h0, h1 = 2 * p, 2 * p + 1
            qd_p = qd[:, lanes]
            kd_p = kd[:, lanes]
            kd_bd = jnp.concatenate(
                [jnp.where(first_head, kd_p, 0.0), jnp.where(first_head, 0.0, kd_p)], axis=0
            ).astype(BF16)
            scores = lax.dot_general(qd_p, kd_bd, _NT, preferred_element_type=F32)
            a = jnp.where(causal, scores, 0.0).astype(BF16)
            v0 = vb[:, h0 * GLA_DV:(h0 + 1) * GLA_DV]
            v1 = vb[:, h1 * GLA_DV:(h1 + 1) * GLA_DV]
            s0 = state_ref[h0 * GLA_DK:(h0 + 1) * GLA_DK, :]
            s1 = state_ref[h1 * GLA_DK:(h1 + 1) * GLA_DK, :]
            s0b = s0.astype(BF16)
            s1b = s1.astype(BF16)
            rhs = jnp.concatenate([
                jnp.concatenate([v0, zeros_v], axis=1),
                jnp.concatenate([zeros_v, v1], axis=1),
                jnp.concatenate([s0b, zeros_v], axis=1),
                jnp.concatenate([zeros_v, s1b], axis=1)], axis=0)
            o_write(c, p, _dot(jnp.concatenate([a, qd_p], axis=1), rhs))
            d_state = lax.dot_general(kte[:, lanes], vb[:, p * 2 * GLA_DV:(p + 1) * 2 * GLA_DV], _TN,
                                      preferred_element_type=F32)
            dec0 = dec_t[p][:GLA_DK, c:c + 1]
            dec1 = dec_t[p][GLA_DK:, c:c + 1]
            state_ref[h0 * GLA_DK:(h0 + 1) * GLA_DK, :] = s0 * dec0 + d_state[:GLA_DK, :GLA_DV]
            state_ref[h1 * GLA_DK:(h1 + 1) * GLA_DK, :] = s1 * dec1 + d_state[GLA_DK:, GLA_DV:]


def _fwd_sweep_kernel(x_ref, n1g_ref, win_ref, wdec_ref, bdec_ref, lng_ref, lnb_ref, wsp_ref, bsp_ref,
                      qkvg_ref, lr_ref, of_ref, yb_ref, state_ref, b_scr):
    @pl.when(pl.program_id(1) == 0)
    def _():
        state_ref[...] = jnp.zeros_like(state_ref)

    h = _rms(x_ref[0], n1g_ref[...]).astype(BF16)
    p = _dot(h, win_ref[...])
    qkvg_ref[0] = p[:, :QKVG_WIDTH].astype(BF16)
    lr = p[:, COL_LR:PROJ_PAD].astype(BF16)
    lr_ref[0] = lr

    la = _log_sigmoid(_dot(lr, wdec_ref[...]) + bdec_ref[...]) * (1.0 / GLA_TAU)

    def o_write(c, pair, val):
        of_ref[0, c * GLA_CHUNK:(c + 1) * GLA_CHUNK, pair * 2 * GLA_DV:(pair + 1) * 2 * GLA_DV] = val

    _gla_tile(p[:, COL_Q:COL_K], p[:, COL_K:COL_V], p[:, COL_V:COL_G], la, state_ref, b_scr, o_write,
              reverse=False)

    huv = p[:, COL_UV:COL_LR]
    z = 0.5 * huv * (1.0 + lax.erf(huv * (0.5 ** 0.5)))
    u = z[:, :GMLP_WIDTH]
    vg = z[:, GMLP_WIDTH:]
    mu = jnp.mean(vg, axis=-1, keepdims=True)
    vc = vg - mu
    vn = (vc * lax.rsqrt(jnp.mean(vc * vc, axis=-1, keepdims=True) + EPS) * lng_ref[...] + lnb_ref[...]).astype(BF16)
    for n in range(u.shape[0] // GMLP_CHUNK):
        rows = slice(n * GMLP_CHUNK, (n + 1) * GMLP_CHUNK)
        for g in range(GMLP_GROUPS):
            cols = slice(g * GMLP_GROUP_DIM, (g + 1) * GMLP_GROUP_DIM)
            s = _dot(wsp_ref[g], vn[rows, cols]) + bsp_ref[g]
            yb_ref[0, rows, cols] = (u[rows, cols] * s).astype(BF16)


def _bwd_sweep_kernel(qkvg_ref, lr_ref, of_ref, yb_ref, x_ref, wdec_ref, bdec_ref, gng_ref, wout_ref,
                      x2_ref, state_ref, b_scr, o_scr):
    @pl.when(pl.program_id(1) == 0)
    def _():
        state_ref[...] = jnp.zeros_like(state_ref)

    la = _log_sigmoid(_dot(lr_ref[0], wdec_ref[...]) + bdec_ref[...]) * (1.0 / GLA_TAU)

    def o_write(c, pair, val):
        o_scr[c * GLA_CHUNK:(c + 1) * GLA_CHUNK, pair * 2 * GLA_DV:(pair + 1) * 2 * GLA_DV] = val

    _gla_tile(qkvg_ref[0, :, COL_Q:COL_K].astype(F32), qkvg_ref[0, :, COL_K:COL_V].astype(F32),
              qkvg_ref[0, :, COL_V:COL_G].astype(F32), la, state_ref, b_scr, o_write, reverse=True)

    gate = qkvg_ref[0, :, COL_G:COL_UV].astype(F32)
    gate = gate * (1.0 / (1.0 + jnp.exp(-gate)))
    ya = []
    for hd in range(GLA_HEADS):
        cols = slice(hd * GLA_DV, (hd + 1) * GLA_DV)
        o = of_ref[0, :, cols] + o_scr[:, cols]
        o = o * lax.rsqrt(jnp.mean(o * o, axis=-1, keepdims=True) + EPS)
        ya.append((o * gng_ref[:, cols] * gate[:, cols]).astype(BF16))
    y = jnp.concatenate(ya + [yb_ref[0]], axis=1)
    x2_ref[0] = x_ref[0] + _dot(y, wout_ref[...])


def _ffn_kernel(x_ref, n2g_ref, wg_ref, wu_ref, wd_ref, fng_ref, o_ref, *, final_norm):
    x = x_ref[...]
    h = _rms(x, n2g_ref[...]).astype(BF16)
    g = _dot(h, wg_ref[...])
    u = _dot(h, wu_ref[...])
    a = (g * (1.0 / (1.0 + jnp.exp(-g))) * u).astype(BF16)
    y = x + _dot(a, wd_ref[...])
    if final_norm:
        y = _rms(y, fng_ref[...])
    o_ref[...] = y


def _const_spec(shape):
    return pl.BlockSpec(shape, lambda *_: (0,) * len(shape))


def _params(semantics):
    return pltpu.CompilerParams(dimension_semantics=semantics, vmem_limit_bytes=VMEM_LIMIT_BYTES)


def _layer(x, n1g, w_in, wdf, bdf, wdb, bdb, gng, lng, lnb, wsp, bsp, w_out, n2g, wg, wu, wd, fng, final_norm):
    B, S, D = x.shape
    ts = min(SEQ_TILE, S)
    nt = S // ts
    assert S % ts == 0 and ts % GMLP_CHUNK == 0 and D == D_MODEL

    w_in_r = jnp.concatenate([
        w_in[:, :COL_UV],
        w_in[:, COL_UV + 2 * GLA_LOWRANK:],
        w_in[:, COL_UV:COL_UV + 2 * GLA_LOWRANK],
        jnp.zeros((D, PROJ_PAD - COL_LR - 2 * GLA_LOWRANK), w_in.dtype)], axis=1).astype(BF16)
    zpad = jnp.zeros((LANES - 2 * GLA_LOWRANK, GLA_KEY_WIDTH), F32)
    zlr = jnp.zeros((GLA_LOWRANK, GLA_KEY_WIDTH), F32)
    wdf_p = jnp.concatenate([wdf, zlr, zpad], axis=0).astype(BF16)
    wdb_p = jnp.concatenate([zlr, wdb, zpad], axis=0).astype(BF16)

    tile = lambda w: pl.BlockSpec((1, ts, w), lambda b, t: (b, t, 0))
    rtile = lambda w: pl.BlockSpec((1, ts, w), lambda b, t: (b, nt - 1 - t, 0))
    sweep_scratch = [pltpu.VMEM((GLA_KEY_WIDTH, GLA_DV), F32), pltpu.VMEM((GLA_HEADS // 2, ts, LANES), F32)]

    qkvg, lr, o_f, y_b = pl.pallas_call(
        _fwd_sweep_kernel,
        grid=(B, nt),
        in_specs=[tile(D), _const_spec((1, D)), _const_spec((D, PROJ_PAD)), _const_spec((LANES, GLA_KEY_WIDTH)),
                  _const_spec((1, GLA_KEY_WIDTH)), _const_spec((1, GMLP_WIDTH)), _const_spec((1, GMLP_WIDTH)),
                  _const_spec((GMLP_GROUPS, GMLP_CHUNK, GMLP_CHUNK)), _const_spec((GMLP_GROUPS, GMLP_CHUNK, 1))],
        out_specs=[tile(QKVG_WIDTH), tile(LANES), tile(GLA_WIDTH), tile(GMLP_WIDTH)],
        out_shape=[jax.ShapeDtypeStruct((B, S, QKVG_WIDTH), BF16), jax.ShapeDtypeStruct((B, S, LANES), BF16),
                   jax.ShapeDtypeStruct((B, S, GLA_WIDTH), F32), jax.ShapeDtypeStruct((B, S, GMLP_WIDTH), BF16)],
        scratch_shapes=sweep_scratch,
        compiler_params=_params(("arbitrary", "arbitrary")),
        name="fwd_sweep",
    )(x, n1g.reshape(1, D), w_in_r, wdf_p, bdf.reshape(1, -1), lng.reshape(1, -1), lnb.reshape(1, -1),
      wsp.astype(BF16), bsp.reshape(GMLP_GROUPS, GMLP_CHUNK, 1))

    x2 = pl.pallas_call(
        _bwd_sweep_kernel,
        grid=(B, nt),
        in_specs=[rtile(QKVG_WIDTH), rtile(LANES), rtile(GLA_WIDTH), rtile(GMLP_WIDTH), rtile(D),
                  _const_spec((LANES, GLA_KEY_WIDTH)), _const_spec((1, GLA_KEY_WIDTH)),
                  _const_spec((1, GLA_WIDTH)), _const_spec((D, D))],
        out_specs=rtile(D),
        out_shape=jax.ShapeDtypeStruct((B, S, D), F32),
        scratch_shapes=sweep_scratch + [pltpu.VMEM((ts, GLA_WIDTH), F32)],
        compiler_params=_params(("arbitrary", "arbitrary")),
        name="bwd_sweep",
    )(qkvg, lr, o_f, y_b, x, wdb_p, bdb.reshape(1, -1), gng.reshape(1, -1), w_out.astype(BF16))

    tm = min(FFN_TILE, B * S)
    rows = pl.BlockSpec((tm, D), lambda i: (i, 0))
    out = pl.pallas_call(
        functools.partial(_ffn_kernel, final_norm=final_norm),
        grid=(B * S // tm,),
        in_specs=[rows, _const_spec((1, D)), _const_spec((D, D_FF)), _const_spec((D, D_FF)),
                  _const_spec((D_FF, D)), _const_spec((1, D))],
        out_specs=rows,
        out_shape=jax.ShapeDtypeStruct((B * S, D), F32),
        compiler_params=_params(("arbitrary",)),
        name="channel_mixer",
    )(x2.reshape(B * S, D), n2g.reshape(1, D), wg.astype(BF16), wu.astype(BF16), wd.astype(BF16), fng.reshape(1, D))
    return out.reshape(B, S, D)


def kernel(x, norm1_g, w_in, w_decay_f, b_decay_f, w_decay_b, b_decay_b, gla_norm_g, gmlp_ln_g, gmlp_ln_b,
           w_spatial, b_spatial, w_out, norm2_g, w_gate, w_up, w_down, final_norm_g):
    depth = norm1_g.shape[0]
    for l in range(depth):
        x = _layer(x, norm1_g[l], w_in[l], w_decay_f[l], b_decay_f[l], w_decay_b[l], b_decay_b[l],
                   gla_norm_g[l], gmlp_ln_g[l], gmlp_ln_b[l], w_spatial[l], b_spatial[l], w_out[l],
                   norm2_g[l], w_gate[l], w_up[l], w_down[l], final_norm_g, final_norm=(l == depth - 1))
    return x
```

```python
import functools

import jax
import jax.numpy as jnp
from jax import lax
from jax.experimental import pallas as pl
from jax.experimental.pallas import tpu as pltpu

F32 = jnp.float32
BF16 = jnp.bfloat16

D_MODEL = 1024
GLA_HEADS = 4
GLA_PAIRS = GLA_HEADS // 2
GLA_DK = 64
GLA_DV = 128
GLA_KEY_WIDTH = GLA_HEADS * GLA_DK
GLA_WIDTH = GLA_HEADS * GLA_DV
GLA_LOWRANK = 16
GLA_TAU = 16.0
GLA_CHUNK = 64
GMLP_GROUPS = 4
GMLP_GROUP_DIM = 128
GMLP_WIDTH = GMLP_GROUPS * GMLP_GROUP_DIM
GMLP_CHUNK = 128
D_FF = 2816
EPS = 1e-6

LANES = 128
MXU_TILE = 256
DOT_COLS = 2 * MXU_TILE
COL_Q = 0
COL_K = COL_Q + GLA_KEY_WIDTH
COL_V = COL_K + GLA_KEY_WIDTH
COL_G = COL_V + GLA_WIDTH
COL_UV = COL_G + GLA_WIDTH
COL_LR = COL_UV + 2 * GMLP_WIDTH
PROJ_PAD = COL_LR + LANES
QKVG_WIDTH = COL_UV

SEQ_TILE = 512
HALF_TILE = SEQ_TILE // 2
N_CHUNKS = SEQ_TILE // GLA_CHUNK
N_FF = D_FF // MXU_TILE
VMEM_LIMIT_BYTES = 60 * 1024 * 1024

_NT = (((1,), (1,)), ((), ()))
_TN = (((0,), (0,)), ((), ()))


def _dot(a, b):
    return jnp.dot(a, b, preferred_element_type=F32)


def _rms(x, g):
    return x * lax.rsqrt(jnp.mean(x * x, axis=-1, keepdims=True) + EPS) * g


def _log_sigmoid(z):
    return jnp.minimum(z, 0.0) - jnp.log(1.0 + jnp.exp(-jnp.abs(z)))


def _sigmoid(z):
    return 1.0 / (1.0 + jnp.exp(-z))


def _gelu(x):
    return 0.5 * x * (1.0 + lax.erf(x * (0.5 ** 0.5)))


def _interleave(*streams):
    totals = [float(sum(c for c, _ in s)) or 1.0 for s in streams]
    pos = [0] * len(streams)
    done = [0.0] * len(streams)
    while True:
        live = [k for k in range(len(streams)) if pos[k] < len(streams[k])]
        if not live:
            return
        k = min(live, key=lambda k: (done[k] + 0.5 * streams[k][pos[k]][0]) / totals[k])
        cost, thunk = streams[k][pos[k]]
        thunk()
        done[k] += cost
        pos[k] += 1


def _skewed(first, second, order):
    seq = []
    for k, item in enumerate(order):
        seq.append(first(item))
        if k >= 1:
            seq.append(second(order[k - 1]))
    seq.append(second(order[-1]))
    return seq


def _gla_stages(read, lr_read, wdec_ref, bdec_ref, state_ref, b_scr, dec_scr, o_write, reverse):
    last = 0 if reverse else GLA_CHUNK - 1
    keep = {}

    def decay_dot(r0):
        def f():
            keep["z", r0] = _dot(lr_read(slice(r0, r0 + HALF_TILE)), wdec_ref[...])
        return (200, f)

    def decay_cumsum(r0):
        def f():
            la = _log_sigmoid(keep.pop(("z", r0)) + bdec_ref[...]) * (1.0 / GLA_TAU)
            r = lax.broadcasted_iota(jnp.int32, (GLA_CHUNK, GLA_CHUNK), 0)
            c = lax.broadcasted_iota(jnp.int32, (GLA_CHUNK, GLA_CHUNK), 1)
            tri = jnp.where((c >= r) if reverse else (c <= r), 1.0, 0.0).astype(BF16)
            for k in range(HALF_TILE // GLA_CHUNK):
                la_c = la[k * GLA_CHUNK:(k + 1) * GLA_CHUNK, :]
                hi = la_c.astype(BF16)
                lo = (la_c - hi.astype(F32)).astype(BF16)
                b_c = _dot(tri, hi) + _dot(tri, lo)
                rows = slice(r0 + k * GLA_CHUNK, r0 + (k + 1) * GLA_CHUNK)
                for p in range(GLA_PAIRS):
                    b_scr[p, rows, :] = b_c[:, p * LANES:(p + 1) * LANES]
        return (450, f)

    def chunk_decays():
        pad = jnp.zeros((LANES - N_CHUNKS, LANES), F32)
        for p in range(GLA_PAIRS):
            b_last = b_scr[p, pl.ds(last, N_CHUNKS, stride=GLA_CHUNK), :]
            dec_scr[p] = jnp.exp(jnp.concatenate([b_last, pad], axis=0).T)

    halves = [HALF_TILE, 0] if reverse else [0, HALF_TILE]
    decay_stages = [decay_dot(r0) for r0 in halves] + [decay_cumsum(r0) for r0 in halves] + [(100, chunk_decays)]

    def first(c):
        def f():
            ci = lax.broadcasted_iota(jnp.int32, (GLA_CHUNK, LANES), 1)
            first_head = ci < GLA_DK
            rows = slice(c * GLA_CHUNK, (c + 1) * GLA_CHUNK)
            for p in range(GLA_PAIRS):
                b = b_scr[p, rows, :]
                e_neg = jnp.exp(-b)
                e_last = jnp.exp(b[last:last + 1, :])
                qd = (read(rows, slice(COL_Q + p * LANES, COL_Q + (p + 1) * LANES))
                      * (jnp.exp(b) * GLA_DK ** -0.5)).astype(BF16)
                kd = read(rows, slice(COL_K + p * LANES, COL_K + (p + 1) * LANES)) * e_neg
                kte = (kd * e_last).astype(BF16)
                vb = read(rows, slice(COL_V + p * 2 * GLA_DV, COL_V + (p + 1) * 2 * GLA_DV)).astype(BF16)
                kd_bd = jnp.concatenate(
                    [jnp.where(first_head, kd, 0.0), jnp.where(first_head, 0.0, kd)], axis=0
                ).astype(BF16)
                scores = lax.dot_general(qd, kd_bd, _NT, preferred_element_type=F32)
                d_state = lax.dot_general(kte, vb, _TN, preferred_element_type=F32)
                keep[c, p] = (qd, vb, scores, d_state)
        return (260, f)

    def second(c):
        def f():
            ri = lax.broadcasted_iota(jnp.int32, (GLA_CHUNK, LANES), 0)
            ti = lax.broadcasted_iota(jnp.int32, (GLA_CHUNK, LANES), 1) & (GLA_DK - 1)
            causal = (ti >= ri) if reverse else (ti <= ri)
            zeros_v = jnp.zeros((GLA_CHUNK, GLA_DV), BF16)
            for p in range(GLA_PAIRS):
                qd, vb, scores, d_state = keep.pop((c, p))
                h0, h1 = 2 * p, 2 * p + 1
                a = jnp.where(causal, scores, 0.0).astype(BF16)
                s0 = state_ref[h0 * GLA_DK:(h0 + 1) * GLA_DK, :]
                s1 = state_ref[h1 * GLA_DK:(h1 + 1) * GLA_DK, :]
                rhs = jnp.concatenate([
                    jnp.concatenate([vb[:, :GLA_DV], zeros_v], axis=1),
                    jnp.concatenate([zeros_v, vb[:, GLA_DV:]], axis=1),
                    jnp.concatenate([s0.astype(BF16), zeros_v], axis=1),
                    jnp.concatenate([zeros_v, s1.astype(BF16)], axis=1)], axis=0)
                o_write(c, p, _dot(jnp.concatenate([a, qd], axis=1), rhs))
                dec0 = dec_scr[p, :GLA_DK, c:c + 1]
                dec1 = dec_scr[p, GLA_DK:, c:c + 1]
                state_ref[h0 * GLA_DK:(h0 + 1) * GLA_DK, :] = s0 * dec0 + d_state[:GLA_DK, :GLA_DV]
                state_ref[h1 * GLA_DK:(h1 + 1) * GLA_DK, :] = s1 * dec1 + d_state[GLA_DK:, GLA_DV:]
        return (120, f)

    return decay_stages, first, second


def _fwd_sweep_kernel(x_ref, n1g_ref, win_ref, wdec_ref, bdec_ref, lng_ref, lnb_ref, wsp_ref, bsp_ref,
                      qkvg_ref, lr_ref, of_ref, yb_ref,
                      p_even, p_odd, h_scr, state_ref, b_scr, dec_scr, *, tiles_per_seq):
    i = pl.program_id(0)

    def project_stages(p_w):
        def norm(r0):
            def f():
                h_scr[r0:r0 + HALF_TILE, :] = _rms(x_ref[r0:r0 + HALF_TILE, :], n1g_ref[...]).astype(BF16)
            return (300, f)

        def piece(r0, lo, hi):
            def f():
                rows = slice(r0, r0 + HALF_TILE)
                val = _dot(h_scr[rows, :], win_ref[:, lo:hi])
                p_w[rows, lo:hi] = val
                if hi <= QKVG_WIDTH:
                    qkvg_ref[rows, lo:hi] = val.astype(BF16)
                elif lo == COL_LR:
                    lr_ref[rows, :] = val.astype(BF16)
            return (hi - lo, f)

        stages = [norm(0), norm(HALF_TILE)]
        for lo in range(0, PROJ_PAD, DOT_COLS):
            for r0 in (0, HALF_TILE):
                stages.append(piece(r0, lo, min(lo + DOT_COLS, PROJ_PAD)))
        return stages

    def mix_stages(p_r):
        def o_write(c, pair, val):
            of_ref[c * GLA_CHUNK:(c + 1) * GLA_CHUNK, pair * 2 * GLA_DV:(pair + 1) * 2 * GLA_DV] = val

        decay_stages, first, second = _gla_stages(
            lambda rows, cols: p_r[rows, cols], lambda rows: p_r[rows, COL_LR:PROJ_PAD].astype(BF16),
            wdec_ref, bdec_ref, state_ref, b_scr, dec_scr, o_write, reverse=False)

        keep = {}

        def gmlp_first(n):
            def f():
                rows = slice(n * GMLP_CHUNK, (n + 1) * GMLP_CHUNK)
                vg = _gelu(p_r[rows, COL_UV + GMLP_WIDTH:COL_LR])
                mu = jnp.mean(vg, axis=-1, keepdims=True)
                vc = vg - mu
                vn = (vc * lax.rsqrt(jnp.mean(vc * vc, axis=-1, keepdims=True) + EPS) * lng_ref[...]
                      + lnb_ref[...]).astype(BF16)
                keep[n] = [_dot(wsp_ref[g], vn[:, g * GMLP_GROUP_DIM:(g + 1) * GMLP_GROUP_DIM])
                           for g in range(GMLP_GROUPS)]
            return (350, f)

        def gmlp_second(n):
            def f():
                rows = slice(n * GMLP_CHUNK, (n + 1) * GMLP_CHUNK)
                for g, s in enumerate(keep.pop(n)):
                    cols = slice(g * GMLP_GROUP_DIM, (g + 1) * GMLP_GROUP_DIM)
                    u = _gelu(p_r[rows, COL_UV + g * GMLP_GROUP_DIM:COL_UV + (g + 1) * GMLP_GROUP_DIM])
                    yb_ref[rows, cols] = (u * (s + bsp_ref[g])).astype(BF16)
            return (200, f)

        return (decay_stages + _skewed(first, second, list(range(N_CHUNKS)))
                + _skewed(gmlp_first, gmlp_second, list(range(SEQ_TILE // GMLP_CHUNK))))

    @pl.when(i == 0)
    def _():
        p_odd[...] = jnp.zeros_like(p_odd)

    @pl.when(jnp.logical_or(i == 0, lax.rem(i - 1, tiles_per_seq) == 0))
    def _():
        state_ref[...] = jnp.zeros_like(state_ref)

    @pl.when(lax.rem(i, 2) == 0)
    def _():
        _interleave(project_stages(p_even), mix_stages(p_odd))

    @pl.when(lax.rem(i, 2) == 1)
    def _():
        _interleave(project_stages(p_odd), mix_stages(p_even))


def _bwd_sweep_kernel(qkvg_ref, lr_ref, of_ref, yb_ref, x_ref, wdec_ref, bdec_ref, gng_ref, wout_ref,
                      n2g_ref, wgu_ref, wd_ref, fng_ref,
                      out_ref,
                      x2_even, x2_odd, h2_scr, a_scr, acc_scr, o_scr, y_scr, state_ref, b_scr, dec_scr,
                      *, tiles_per_seq, final_norm):
    j = pl.program_id(0)

    def mix_stages(x2_w):
        def o_write(c, pair, val):
            o_scr[c * GLA_CHUNK:(c + 1) * GLA_CHUNK, pair * 2 * GLA_DV:(pair + 1) * 2 * GLA_DV] = val

        decay_stages, first, second = _gla_stages(
            lambda rows, cols: qkvg_ref[rows, cols].astype(F32), lambda rows: lr_ref[rows, :],
            wdec_ref, bdec_ref, state_ref, b_scr, dec_scr, o_write, reverse=True)

        def gate(r0):
            def f():
                rows = slice(r0, r0 + HALF_TILE)
                g = qkvg_ref[rows, COL_G:COL_UV].astype(F32)
                g = g * _sigmoid(g)
                for hd in range(GLA_HEADS):
                    cols = slice(hd * GLA_DV, (hd + 1) * GLA_DV)
                    o = of_ref[rows, cols] + o_scr[rows, cols]
                    o = o * lax.rsqrt(jnp.mean(o * o, axis=-1, keepdims=True) + EPS)
                    y_scr[rows, cols] = (o * gng_ref[:, cols] * g[:, cols]).astype(BF16)
                y_scr[rows, GLA_WIDTH:] = yb_ref[rows, :]
            return (250, f)

        def out_proj(r0, lo):
            def f():
                rows = slice(r0, r0 + HALF_TILE)
                x2_w[rows, lo:lo + DOT_COLS] = (x_ref[rows, lo:lo + DOT_COLS]
                                                + _dot(y_scr[rows, :], wout_ref[:, lo:lo + DOT_COLS]))
            return (512, f)

        chunks = _skewed(first, second, list(range(N_CHUNKS - 1, -1, -1)))
        per_half = len(chunks) // 2
        upper = [gate(HALF_TILE)] + [out_proj(HALF_TILE, lo) for lo in range(0, D_MODEL, DOT_COLS)]
        lower = [gate(0)] + [out_proj(0, lo) for lo in range(0, D_MODEL, DOT_COLS)]
        tail = list(chunks[per_half + 1:])
        stages = decay_stages + chunks[:per_half + 1]
        for k, st in enumerate(upper):
            stages.append(st)
            stages += tail[2 * k:2 * k + 2]
        stages += tail[2 * len(upper):]
        return stages + lower

    def ffn_stages(x2_r):
        def norm(r0):
            def f():
                h2_scr[r0:r0 + HALF_TILE, :] = _rms(x2_r[r0:r0 + HALF_TILE, :], n2g_ref[...]).astype(BF16)
            return (300, f)

        def up(c):
            def f():
                gu = _dot(h2_scr[...], wgu_ref[:, 2 * c * MXU_TILE:2 * (c + 1) * MXU_TILE])
                g = gu[:, :MXU_TILE]
                a_scr[c % 2] = (g * _sigmoid(g) * gu[:, MXU_TILE:]).astype(BF16)
            return (1024, f)

        def down(c):
            def f():
                d = _dot(a_scr[c % 2], wd_ref[c * MXU_TILE:(c + 1) * MXU_TILE, :])
                if c == 0:
                    acc_scr[...] = d
                else:
                    acc_scr[...] += d
            return (512, f)

        def final(r0):
            def f():
                rows = slice(r0, r0 + HALF_TILE)
                y = x2_r[rows, :] + acc_scr[rows, :]
                if final_norm:
                    y = _rms(y, fng_ref[...])
                out_ref[rows, :] = y
            return (300, f)

        return ([norm(0), norm(HALF_TILE)] + _skewed(up, down, list(range(N_FF)))
                + [final(0), final(HALF_TILE)])

    @pl.when(j == 0)
    def _():
        x2_odd[...] = jnp.zeros_like(x2_odd)

    @pl.when(lax.rem(j, tiles_per_seq) == 0)
    def _():
        state_ref[...] = jnp.zeros_like(state_ref)

    @pl.when(lax.rem(j, 2) == 0)
    def _():
        _interleave(ffn_stages(x2_odd), mix_stages(x2_even))

    @pl.when(lax.rem(j, 2) == 1)
    def _():
        _interleave(ffn_stages(x2_even), mix_stages(x2_odd))


def _const_spec(shape):
    return pl.BlockSpec(shape, lambda *_: (0,) * len(shape), pipeline_mode=pl.Buffered(1))


def _params():
    return pltpu.CompilerParams(dimension_semantics=("arbitrary",), vmem_limit_bytes=VMEM_LIMIT_BYTES)


def _layer(x, n1g, w_in, wdf, bdf, wdb, bdb, gng, lng, lnb, wsp, bsp, w_out, n2g, wg, wu, wd, fng, final_norm):
    B, S, D = x.shape
    ts = SEQ_TILE
    assert S % ts == 0 and D == D_MODEL
    nt = S // ts
    n = B * nt
    x2d = x.reshape(B * S, D)

    w_in_r = jnp.concatenate([
        w_in[:, :COL_UV],
        w_in[:, COL_UV + 2 * GLA_LOWRANK:],
        w_in[:, COL_UV:COL_UV + 2 * GLA_LOWRANK],
        jnp.zeros((D, PROJ_PAD - COL_LR - 2 * GLA_LOWRANK), w_in.dtype)], axis=1).astype(BF16)
    zpad = jnp.zeros((LANES - 2 * GLA_LOWRANK, GLA_KEY_WIDTH), F32)
    zlr = jnp.zeros((GLA_LOWRANK, GLA_KEY_WIDTH), F32)
    wdf_p = jnp.concatenate([wdf, zlr, zpad], axis=0).astype(BF16)
    wdb_p = jnp.concatenate([zlr, wdb, zpad], axis=0).astype(BF16)
    w_gu = jnp.concatenate([wg.reshape(D, N_FF, MXU_TILE), wu.reshape(D, N_FF, MXU_TILE)],
                           axis=2).reshape(D, 2 * D_FF).astype(BF16)

    cur = lambda w: pl.BlockSpec((ts, w), lambda i: (jnp.minimum(i, n - 1), 0))
    prev = lambda w: pl.BlockSpec((ts, w), lambda i: (jnp.maximum(i - 1, 0), 0))
    gla_scratch = [pltpu.VMEM((GLA_KEY_WIDTH, GLA_DV), F32), pltpu.VMEM((GLA_PAIRS, ts, LANES), F32),
                   pltpu.VMEM((GLA_PAIRS, LANES, LANES), F32)]
    qkvg, lr, o_f, y_b = pl.pallas_call(
        functools.partial(_fwd_sweep_kernel, tiles_per_seq=nt),
        grid=(n + 1,),
        in_specs=[cur(D), _const_spec((1, D)), _const_spec((D, PROJ_PAD)), _const_spec((LANES, GLA_KEY_WIDTH)),
                  _const_spec((1, GLA_KEY_WIDTH)), _const_spec((1, GMLP_WIDTH)), _const_spec((1, GMLP_WIDTH)),
                  _const_spec((GMLP_GROUPS, GMLP_CHUNK, GMLP_CHUNK)), _const_spec((GMLP_GROUPS, GMLP_CHUNK, 1))],
        out_specs=[cur(QKVG_WIDTH), cur(LANES), prev(GLA_WIDTH), prev(GMLP_WIDTH)],
        out_shape=[jax.ShapeDtypeStruct((B * S, QKVG_WIDTH), BF16), jax.ShapeDtypeStruct((B * S, LANES), BF16),
                   jax.ShapeDtypeStruct((B * S, GLA_WIDTH), F32), jax.ShapeDtypeStruct((B * S, GMLP_WIDTH), BF16)],
        scratch_shapes=[pltpu.VMEM((ts, PROJ_PAD), F32), pltpu.VMEM((ts, PROJ_PAD), F32),
                        pltpu.VMEM((ts, D), BF16)] + gla_scratch,
        compiler_params=_params(),
        name="fwd_sweep",
    )(x2d, n1g.reshape(1, D), w_in_r, wdf_p, bdf.reshape(1, -1), lng.reshape(1, -1), lnb.reshape(1, -1),
      wsp.astype(BF16), bsp.reshape(GMLP_GROUPS, GMLP_CHUNK, 1))

    def rblock(j):
        j = jnp.clip(j, 0, n - 1)
        return (j // nt) * nt + (nt - 1 - j % nt)

    rcur = lambda w: pl.BlockSpec((ts, w), lambda j: (rblock(j), 0))
    rprev = lambda w: pl.BlockSpec((ts, w), lambda j: (rblock(j - 1), 0))
    out = pl.pallas_call(
        functools.partial(_bwd_sweep_kernel, tiles_per_seq=nt, final_norm=final_norm),
        grid=(n + 1,),
        in_specs=[rcur(QKVG_WIDTH), rcur(LANES), rcur(GLA_WIDTH), rcur(GMLP_WIDTH), rcur(D),
                  _const_spec((LANES, GLA_KEY_WIDTH)), _const_spec((1, GLA_KEY_WIDTH)),
                  _const_spec((1, GLA_WIDTH)), _const_spec((D, D)),
                  _const_spec((1, D)), _const_spec((D, 2 * D_FF)), _const_spec((D_FF, D)), _const_spec((1, D))],
        out_specs=rprev(D),
        out_shape=jax.ShapeDtypeStruct((B * S, D), F32),
        scratch_shapes=[pltpu.VMEM((ts, D), F32), pltpu.VMEM((ts, D), F32), pltpu.VMEM((ts, D), BF16),
                        pltpu.VMEM((2, ts, MXU_TILE), BF16), pltpu.VMEM((ts, D), F32),
                        pltpu.VMEM((ts, GLA_WIDTH), F32), pltpu.VMEM((ts, D), BF16)] + gla_scratch,
        compiler_params=_params(),
        name="bwd_sweep",
    )(qkvg, lr, o_f, y_b, x2d, wdb_p, bdb.reshape(1, -1), gng.reshape(1, -1), w_out.astype(BF16),
      n2g.reshape(1, D), w_gu, wd.astype(BF16), fng.reshape(1, D))
    return out.reshape(B, S, D)


def kernel(x, norm1_g, w_in, w_decay_f, b_decay_f, w_decay_b, b_decay_b, gla_norm_g, gmlp_ln_g, gmlp_ln_b,
           w_spatial, b_spatial, w_out, norm2_g, w_gate, w_up, w_down, final_norm_g):
    depth = norm1_g.shape[0]
    for l in range(depth):
        x = _layer(x, norm1_g[l], w_in[l], w_decay_f[l], b_decay_f[l], w_decay_b[l], b_decay_b[l],
                   gla_norm_g[l], gmlp_ln_g[l], gmlp_ln_b[l], w_spatial[l], b_spatial[l], w_out[l],
                   norm2_g[l], w_gate[l], w_up[l], w_down[l], final_norm_g, final_norm=(l == depth - 1))
    return x
```

```python
import functools

import jax
import jax.numpy as jnp
from jax import lax
from jax.experimental import pallas as pl
from jax.experimental.pallas import tpu as pltpu

F32 = jnp.float32
BF16 = jnp.bfloat16

D_MODEL = 1024
GLA_HEADS = 4
GLA_PAIRS = GLA_HEADS // 2
GLA_DK = 64
GLA_DV = 128
GLA_KEY_WIDTH = GLA_HEADS * GLA_DK
GLA_WIDTH = GLA_HEADS * GLA_DV
GLA_LOWRANK = 16
GLA_TAU = 16.0
GLA_CHUNK = 64
GMLP_GROUPS = 4
GMLP_GROUP_DIM = 128
GMLP_WIDTH = GMLP_GROUPS * GMLP_GROUP_DIM
GMLP_CHUNK = 128
D_FF = 2816
EPS = 1e-6

LANES = 128
MXU_TILE = 256
DOT_COLS = 2 * MXU_TILE
COL_Q = 0
COL_K = COL_Q + GLA_KEY_WIDTH
COL_V = COL_K + GLA_KEY_WIDTH
COL_G = COL_V + GLA_WIDTH
COL_UV = COL_G + GLA_WIDTH
COL_LR = COL_UV + 2 * GMLP_WIDTH
PROJ_PAD = COL_LR + LANES
QKVG_WIDTH = COL_UV

SEQ_TILE = 512
HALF_TILE = SEQ_TILE // 2
N_CHUNKS = SEQ_TILE // GLA_CHUNK
N_FF = D_FF // MXU_TILE
VMEM_LIMIT_BYTES = 60 * 1024 * 1024

_NT = (((1,), (1,)), ((), ()))
_TN = (((0,), (0,)), ((), ()))


def _dot(a, b):
    return jnp.dot(a, b, preferred_element_type=F32)


def _rms(x, g):
    return x * lax.rsqrt(jnp.mean(x * x, axis=-1, keepdims=True) + EPS) * g


def _log_sigmoid(z):
    return jnp.minimum(z, 0.0) - jnp.log(1.0 + jnp.exp(-jnp.abs(z)))


def _sigmoid(z):
    return 1.0 / (1.0 + jnp.exp(-z))


def _gelu(x):
    return 0.5 * x * (1.0 + lax.erf(x * (0.5 ** 0.5)))


def _interleave(*streams, windows=None):
    windows = windows or [(0.0, 1.0)] * len(streams)
    totals = [float(sum(c for c, _ in s)) or 1.0 for s in streams]
    pos = [0] * len(streams)
    done = [0.0] * len(streams)

    def when(k):
        lo, hi = windows[k]
        return lo + (hi - lo) * (done[k] + 0.5 * streams[k][pos[k]][0]) / totals[k]

    while True:
        live = [k for k in range(len(streams)) if pos[k] < len(streams[k])]
        if not live:
            return
        k = min(live, key=when)
        cost, thunk = streams[k][pos[k]]
        thunk()
        done[k] += cost
        pos[k] += 1


def _skewed(first, second, order):
    seq = []
    for k, item in enumerate(order):
        seq.append(first(item))
        if k >= 1:
            seq.append(second(order[k - 1]))
    seq.append(second(order[-1]))
    return seq


def _gla_stages(read, lr_read, wdec_ref, bdec_ref, state_ref, b_scr, dec_scr, o_write, reverse):
    last = 0 if reverse else GLA_CHUNK - 1
    keep = {}

    def decay_dot(r0):
        def f():
            keep["z", r0] = _dot(lr_read(slice(r0, r0 + HALF_TILE)), wdec_ref[...])
        return (200, f)

    def decay_cumsum(r0):
        def f():
            la = _log_sigmoid(keep.pop(("z", r0)) + bdec_ref[...]) * (1.0 / GLA_TAU)
            r = lax.broadcasted_iota(jnp.int32, (GLA_CHUNK, GLA_CHUNK), 0)
            c = lax.broadcasted_iota(jnp.int32, (GLA_CHUNK, GLA_CHUNK), 1)
            tri = jnp.where((c >= r) if reverse else (c <= r), 1.0, 0.0).astype(BF16)
            for k in range(HALF_TILE // GLA_CHUNK):
                la_c = la[k * GLA_CHUNK:(k + 1) * GLA_CHUNK, :]
                hi = la_c.astype(BF16)
                lo = (la_c - hi.astype(F32)).astype(BF16)
                b_c = _dot(tri, hi) + _dot(tri, lo)
                rows = slice(r0 + k * GLA_CHUNK, r0 + (k + 1) * GLA_CHUNK)
                for p in range(GLA_PAIRS):
                    b_scr[p, rows, :] = b_c[:, p * LANES:(p + 1) * LANES]
        return (450, f)

    def chunk_decays():
        pad = jnp.zeros((LANES - N_CHUNKS, LANES), F32)
        for p in range(GLA_PAIRS):
            b_last = b_scr[p, pl.ds(last, N_CHUNKS, stride=GLA_CHUNK), :]
            dec_scr[p] = jnp.exp(jnp.concatenate([b_last, pad], axis=0).T)

    halves = [HALF_TILE, 0] if reverse else [0, HALF_TILE]
    decay_stages = [decay_dot(r0) for r0 in halves] + [decay_cumsum(r0) for r0 in halves] + [(100, chunk_decays)]

    def first(c):
        def f():
            ci = lax.broadcasted_iota(jnp.int32, (GLA_CHUNK, LANES), 1)
            first_head = ci < GLA_DK
            rows = slice(c * GLA_CHUNK, (c + 1) * GLA_CHUNK)
            for p in range(GLA_PAIRS):
                b = b_scr[p, rows, :]
                e_neg = jnp.exp(-b)
                e_last = jnp.exp(b[last:last + 1, :])
                qd = (read(rows, slice(COL_Q + p * LANES, COL_Q + (p + 1) * LANES))
                      * (jnp.exp(b) * GLA_DK ** -0.5)).astype(BF16)
                kd = read(rows, slice(COL_K + p * LANES, COL_K + (p + 1) * LANES)) * e_neg
                kte = (kd * e_last).astype(BF16)
                vb = read(rows, slice(COL_V + p * 2 * GLA_DV, COL_V + (p + 1) * 2 * GLA_DV)).astype(BF16)
                kd_bd = jnp.concatenate(
                    [jnp.where(first_head, kd, 0.0), jnp.where(first_head, 0.0, kd)], axis=0
                ).astype(BF16)
                scores = lax.dot_general(qd, kd_bd, _NT, preferred_element_type=F32)
                d_state = lax.dot_general(kte, vb, _TN, preferred_element_type=F32)
                keep[c, p] = (qd, vb, scores, d_state)
        return (260, f)

    def second(c):
        def f():
            ri = lax.broadcasted_iota(jnp.int32, (GLA_CHUNK, LANES), 0)
            ti = lax.broadcasted_iota(jnp.int32, (GLA_CHUNK, LANES), 1) & (GLA_DK - 1)
            causal = (ti >= ri) if reverse else (ti <= ri)
            zeros_v = jnp.zeros((GLA_CHUNK, GLA_DV), BF16)
            for p in range(GLA_PAIRS):
                qd, vb, scores, d_state = keep.pop((c, p))
                h0, h1 = 2 * p, 2 * p + 1
                a = jnp.where(causal, scores, 0.0).astype(BF16)
                s0 = state_ref[h0 * GLA_DK:(h0 + 1) * GLA_DK, :]
                s1 = state_ref[h1 * GLA_DK:(h1 + 1) * GLA_DK, :]
                rhs = jnp.concatenate([
                    jnp.concatenate([vb[:, :GLA_DV], zeros_v], axis=1),
                    jnp.concatenate([zeros_v, vb[:, GLA_DV:]], axis=1),
                    jnp.concatenate([s0.astype(BF16), zeros_v], axis=1),
                    jnp.concatenate([zeros_v, s1.astype(BF16)], axis=1)], axis=0)
                o_write(c, p, _dot(jnp.concatenate([a, qd], axis=1), rhs))
                dec0 = dec_scr[p, :GLA_DK, c:c + 1]
                dec1 = dec_scr[p, GLA_DK:, c:c + 1]
                state_ref[h0 * GLA_DK:(h0 + 1) * GLA_DK, :] = s0 * dec0 + d_state[:GLA_DK, :GLA_DV]
                state_ref[h1 * GLA_DK:(h1 + 1) * GLA_DK, :] = s1 * dec1 + d_state[GLA_DK:, GLA_DV:]
        return (120, f)

    return decay_stages, first, second


def _fwd_sweep_kernel(x_ref, n1g_ref, win_ref, wdec_ref, bdec_ref, lng_ref, lnb_ref, wsp_ref, bsp_ref,
                      qkvg_ref, lr_ref, of_ref, yb_ref,
                      p_even, p_odd, h_even, h_odd, state_ref, b_scr, dec_scr, *, tiles_per_seq):
    i = pl.program_id(0)

    def norm_stages(h_w):
        def norm(r0):
            def f():
                h_w[r0:r0 + HALF_TILE, :] = _rms(x_ref[r0:r0 + HALF_TILE, :], n1g_ref[...]).astype(BF16)
            return (300, f)

        return [norm(0), norm(HALF_TILE)]

    def project_stages(h_r, p_w):
        def piece(lo, hi):
            def f():
                val = _dot(h_r[...], win_ref[:, lo:hi])
                p_w[:, lo:hi] = val
                if hi <= QKVG_WIDTH:
                    qkvg_ref[:, lo:hi] = val.astype(BF16)
                elif hi == PROJ_PAD:
                    lr_ref[...] = val[:, COL_LR - lo:].astype(BF16)
            return (2 * (hi - lo), f)

        bounds = list(range(0, COL_LR, DOT_COLS)) + [PROJ_PAD]
        return [piece(lo, hi) for lo, hi in zip(bounds[:-1], bounds[1:])]

    def mix_stages(p_r):
        def o_write(c, pair, val):
            of_ref[c * GLA_CHUNK:(c + 1) * GLA_CHUNK, pair * 2 * GLA_DV:(pair + 1) * 2 * GLA_DV] = val

        decay_stages, first, second = _gla_stages(
            lambda rows, cols: p_r[rows, cols], lambda rows: p_r[rows, COL_LR:PROJ_PAD].astype(BF16),
            wdec_ref, bdec_ref, state_ref, b_scr, dec_scr, o_write, reverse=False)

        keep = {}

        def gmlp_first(n):
            def f():
                rows = slice(n * GMLP_CHUNK, (n + 1) * GMLP_CHUNK)
                vg = _gelu(p_r[rows, COL_UV + GMLP_WIDTH:COL_LR])
                mu = jnp.mean(vg, axis=-1, keepdims=True)
                vc = vg - mu
                vn = (vc * lax.rsqrt(jnp.mean(vc * vc, axis=-1, keepdims=True) + EPS) * lng_ref[...]
                      + lnb_ref[...]).astype(BF16)
                keep[n] = [_dot(wsp_ref[g], vn[:, g * GMLP_GROUP_DIM:(g + 1) * GMLP_GROUP_DIM])
                           for g in range(GMLP_GROUPS)]
            return (350, f)

        def gmlp_second(n):
            def f():
                rows = slice(n * GMLP_CHUNK, (n + 1) * GMLP_CHUNK)
                for g, s in enumerate(keep.pop(n)):
                    cols = slice(g * GMLP_GROUP_DIM, (g + 1) * GMLP_GROUP_DIM)
                    u = _gelu(p_r[rows, COL_UV + g * GMLP_GROUP_DIM:COL_UV + (g + 1) * GMLP_GROUP_DIM])
                    yb_ref[rows, cols] = (u * (s + bsp_ref[g])).astype(BF16)
            return (200, f)

        return (decay_stages + _skewed(first, second, list(range(N_CHUNKS)))
                + _skewed(gmlp_first, gmlp_second, list(range(SEQ_TILE // GMLP_CHUNK))))

    @pl.when(i == 0)
    def _():
        h_odd[...] = jnp.zeros_like(h_odd)
        p_even[...] = jnp.zeros_like(p_even)

    @pl.when(jnp.logical_or(i == 0, lax.rem(i - 2, tiles_per_seq) == 0))
    def _():
        state_ref[...] = jnp.zeros_like(state_ref)

    windows = [(0.1, 0.8), (-0.1, 1.0), (0.0, 0.92)]

    @pl.when(lax.rem(i, 2) == 0)
    def _():
        _interleave(norm_stages(h_even), project_stages(h_odd, p_odd), mix_stages(p_even), windows=windows)

    @pl.when(lax.rem(i, 2) == 1)
    def _():
        _interleave(norm_stages(h_odd), project_stages(h_even, p_even), mix_stages(p_odd), windows=windows)


def _bwd_sweep_kernel(qkvg_ref, lr_ref, of_ref, yb_ref, x_ref, wdec_ref, bdec_ref, gng_ref, wout_ref,
                      n2g_ref, wgu_ref, wd_ref, fng_ref,
                      out_ref,
                      x2_even, x2_odd, h2_even, h2_odd, a_scr, acc_scr, o_scr, y_scr, state_ref, b_scr, dec_scr,
                      *, tiles_per_seq, final_norm):
    j = pl.program_id(0)

    def mix_stages(x2_w, h2_w):
        def o_write(c, pair, val):
            o_scr[c * GLA_CHUNK:(c + 1) * GLA_CHUNK, pair * 2 * GLA_DV:(pair + 1) * 2 * GLA_DV] = val

        decay_stages, first, second = _gla_stages(
            lambda rows, cols: qkvg_ref[rows, cols].astype(F32), lambda rows: lr_ref[rows, :],
            wdec_ref, bdec_ref, state_ref, b_scr, dec_scr, o_write, reverse=True)

        def gate(r0):
            def f():
                rows = slice(r0, r0 + HALF_TILE)
                g = qkvg_ref[rows, COL_G:COL_UV].astype(F32)
                g = g * _sigmoid(g)
                for hd in range(GLA_HEADS):
                    cols = slice(hd * GLA_DV, (hd + 1) * GLA_DV)
                    o = of_ref[rows, cols] + o_scr[rows, cols]
                    o = o * lax.rsqrt(jnp.mean(o * o, axis=-1, keepdims=True) + EPS)
                    y_scr[rows, cols] = (o * gng_ref[:, cols] * g[:, cols]).astype(BF16)
                y_scr[rows, GLA_WIDTH:] = yb_ref[rows, :]
            return (250, f)

        def out_proj(r0, lo):
            def f():
                rows = slice(r0, r0 + HALF_TILE)
                x2_w[rows, lo:lo + DOT_COLS] = (x_ref[rows, lo:lo + DOT_COLS]
                                                + _dot(y_scr[rows, :], wout_ref[:, lo:lo + DOT_COLS]))
            return (512, f)

        def norm(r0):
            def f():
                rows = slice(r0, r0 + HALF_TILE)
                h2_w[rows, :] = _rms(x2_w[rows, :], n2g_ref[...]).astype(BF16)
            return (300, f)

        chunks = _skewed(first, second, list(range(N_CHUNKS - 1, -1, -1)))
        per_half = len(chunks) // 2
        upper = ([gate(HALF_TILE)] + [out_proj(HALF_TILE, lo) for lo in range(0, D_MODEL, DOT_COLS)]
                 + [norm(HALF_TILE)])
        lower = [gate(0)] + [out_proj(0, lo) for lo in range(0, D_MODEL, DOT_COLS)] + [norm(0)]
        tail = list(chunks[per_half + 1:])
        stages = decay_stages + chunks[:per_half + 1]
        for k, st in enumerate(upper):
            stages.append(st)
            stages += tail[2 * k:2 * k + 2]
        stages += tail[2 * len(upper):]
        return stages + lower

    def ffn_stages(x2_r, h2_r):
        def up(c):
            def f():
                gu = _dot(h2_r[...], wgu_ref[:, 2 * c * MXU_TILE:2 * (c + 1) * MXU_TILE])
                g = gu[:, :MXU_TILE]
                a_scr[c % 2] = (g * _sigmoid(g) * gu[:, MXU_TILE:]).astype(BF16)
            return (1024, f)

        def down(c):
            def f():
                d = _dot(a_scr[c % 2], wd_ref[c * MXU_TILE:(c + 1) * MXU_TILE, :])
                if c == 0:
                    acc_scr[...] = d
                else:
                    acc_scr[...] += d
            return (512, f)

        def final(r0):
            def f():
                rows = slice(r0, r0 + HALF_TILE)
                y = x2_r[rows, :] + acc_scr[rows, :]
                if final_norm:
                    y = _rms(y, fng_ref[...])
                out_ref[rows, :] = y
            return (300, f)

        return _skewed(up, down, list(range(N_FF))) + [final(0), final(HALF_TILE)]

    @pl.when(j == 0)
    def _():
        x2_odd[...] = jnp.zeros_like(x2_odd)
        h2_odd[...] = jnp.zeros_like(h2_odd)

    @pl.when(lax.rem(j, tiles_per_seq) == 0)
    def _():
        state_ref[...] = jnp.zeros_like(state_ref)

    windows = [(-0.03, 0.95), (0.0, 1.0)]

    @pl.when(lax.rem(j, 2) == 0)
    def _():
        _interleave(ffn_stages(x2_odd, h2_odd), mix_stages(x2_even, h2_even), windows=windows)

    @pl.when(lax.rem(j, 2) == 1)
    def _():
        _interleave(ffn_stages(x2_even, h2_even), mix_stages(x2_odd, h2_odd), windows=windows)


def _const_spec(shape):
    return pl.BlockSpec(shape, lambda *_: (0,) * len(shape), pipeline_mode=pl.Buffered(1))


def _params():
    return pltpu.CompilerParams(dimension_semantics=("arbitrary",), vmem_limit_bytes=VMEM_LIMIT_BYTES)


def _layer(x, n1g, w_in, wdf, bdf, wdb, bdb, gng, lng, lnb, wsp, bsp, w_out, n2g, wg, wu, wd, fng, final_norm):
    B, S, D = x.shape
    ts = SEQ_TILE
    assert S % ts == 0 and D == D_MODEL
    nt = S // ts
    n = B * nt
    x2d = x.reshape(B * S, D)

    w_in_b = w_in.astype(BF16)
    w_in_r = jnp.concatenate([
        w_in_b[:, :COL_UV],
        w_in_b[:, COL_UV + 2 * GLA_LOWRANK:],
        w_in_b[:, COL_UV:COL_UV + 2 * GLA_LOWRANK],
        jnp.zeros((D, PROJ_PAD - COL_LR - 2 * GLA_LOWRANK), BF16)], axis=1)
    zpad = jnp.zeros((LANES - 2 * GLA_LOWRANK, GLA_KEY_WIDTH), F32)
    zlr = jnp.zeros((GLA_LOWRANK, GLA_KEY_WIDTH), F32)
    wdf_p = jnp.concatenate([wdf, zlr, zpad], axis=0).astype(BF16)
    wdb_p = jnp.concatenate([zlr, wdb, zpad], axis=0).astype(BF16)
    wg_b, wu_b = wg.astype(BF16), wu.astype(BF16)
    w_gu = jnp.concatenate([w[:, c * MXU_TILE:(c + 1) * MXU_TILE] for c in range(N_FF) for w in (wg_b, wu_b)],
                           axis=1)

    tile_spec = lambda w, lag: pl.BlockSpec((ts, w), lambda i: (jnp.clip(i - lag, 0, n - 1), 0))
    gla_scratch = [pltpu.VMEM((GLA_KEY_WIDTH, GLA_DV), F32), pltpu.VMEM((GLA_PAIRS, ts, LANES), F32),
                   pltpu.VMEM((GLA_PAIRS, LANES, LANES), F32)]
    qkvg, lr, o_f, y_b = pl.pallas_call(
        functools.partial(_fwd_sweep_kernel, tiles_per_seq=nt),
        grid=(n + 2,),
        in_specs=[tile_spec(D, 0), _const_spec((1, D)), _const_spec((D, PROJ_PAD)),
                  _const_spec((LANES, GLA_KEY_WIDTH)),
                  _const_spec((1, GLA_KEY_WIDTH)), _const_spec((1, GMLP_WIDTH)), _const_spec((1, GMLP_WIDTH)),
                  _const_spec((GMLP_GROUPS, GMLP_CHUNK, GMLP_CHUNK)), _const_spec((GMLP_GROUPS, GMLP_CHUNK, 1))],
        out_specs=[tile_spec(QKVG_WIDTH, 1), tile_spec(LANES, 1), tile_spec(GLA_WIDTH, 2), tile_spec(GMLP_WIDTH, 2)],
        out_shape=[jax.ShapeDtypeStruct((B * S, QKVG_WIDTH), BF16), jax.ShapeDtypeStruct((B * S, LANES), BF16),
                   jax.ShapeDtypeStruct((B * S, GLA_WIDTH), F32), jax.ShapeDtypeStruct((B * S, GMLP_WIDTH), BF16)],
        scratch_shapes=[pltpu.VMEM((ts, PROJ_PAD), F32), pltpu.VMEM((ts, PROJ_PAD), F32),
                        pltpu.VMEM((ts, D), BF16), pltpu.VMEM((ts, D), BF16)] + gla_scratch,
        compiler_params=_params(),
        name="fwd_sweep",
    )(x2d, n1g.reshape(1, D), w_in_r, wdf_p, bdf.reshape(1, -1), lng.reshape(1, -1), lnb.reshape(1, -1),
      wsp.astype(BF16), bsp.reshape(GMLP_GROUPS, GMLP_CHUNK, 1))

    def rblock(j):
        j = jnp.clip(j, 0, n - 1)
        return (j // nt) * nt + (nt - 1 - j % nt)

    rcur = lambda w: pl.BlockSpec((ts, w), lambda j: (rblock(j), 0))
    rprev = lambda w: pl.BlockSpec((ts, w), lambda j: (rblock(j - 1), 0))
    out = pl.pallas_call(
        functools.partial(_bwd_sweep_kernel, tiles_per_seq=nt, final_norm=final_norm),
        grid=(n + 1,),
        in_specs=[rcur(QKVG_WIDTH), rcur(LANES), rcur(GLA_WIDTH), rcur(GMLP_WIDTH), rcur(D),
                  _const_spec((LANES, GLA_KEY_WIDTH)), _const_spec((1, GLA_KEY_WIDTH)),
                  _const_spec((1, GLA_WIDTH)), _const_spec((D, D)),
                  _const_spec((1, D)), _const_spec((D, 2 * D_FF)), _const_spec((D_FF, D)), _const_spec((1, D))],
        out_specs=rprev(D),
        out_shape=jax.ShapeDtypeStruct((B * S, D), F32),
        scratch_shapes=[pltpu.VMEM((ts, D), F32), pltpu.VMEM((ts, D), F32),
                        pltpu.VMEM((ts, D), BF16), pltpu.VMEM((ts, D), BF16),
                        pltpu.VMEM((2, ts, MXU_TILE), BF16), pltpu.VMEM((ts, D), F32),
                        pltpu.VMEM((ts, GLA_WIDTH), F32), pltpu.VMEM((ts, D), BF16)] + gla_scratch,
        compiler_params=_params(),
        name="bwd_sweep",
    )(qkvg, lr, o_f, y_b, x2d, wdb_p, bdb.reshape(1, -1), gng.reshape(1, -1), w_out.astype(BF16),
      n2g.reshape(1, D), w_gu, wd.astype(BF16), fng.reshape(1, D))
    return out.reshape(B, S, D)


def kernel(x, norm1_g, w_in, w_decay_f, b_decay_f, w_decay_b, b_decay_b, gla_norm_g, gmlp_ln_g, gmlp_ln_b,
           w_spatial, b_spatial, w_out, norm2_g, w_gate, w_up, w_down, final_norm_g):
    depth = norm1_g.shape[0]
    for l in range(depth):
        x = _layer(x, norm1_g[l], w_in[l], w_decay_f[l], b_decay_f[l], w_decay_b[l], b_decay_b[l],
                   gla_norm_g[l], gmlp_ln_g[l], gmlp_ln_b[l], w_spatial[l], b_spatial[l], w_out[l],
                   norm2_g[l], w_gate[l], w_up[l], w_down[l], final_norm_g, final_norm=(l == depth - 1))
    return x
```

```python
import functools

import jax
import jax.numpy as jnp
from jax import lax
from jax.experimental import pallas as pl
from jax.experimental.pallas import tpu as pltpu

F32 = jnp.float32
BF16 = jnp.bfloat16

D_MODEL = 1024
GLA_HEADS = 4
GLA_PAIRS = GLA_HEADS // 2
GLA_DK = 64
GLA_DV = 128
GLA_KEY_WIDTH = GLA_HEADS * GLA_DK
GLA_WIDTH = GLA_HEADS * GLA_DV
GLA_LOWRANK = 16
GLA_TAU = 16.0
GLA_CHUNK = 64
GMLP_GROUPS = 4
GMLP_GROUP_DIM = 128
GMLP_WIDTH = GMLP_GROUPS * GMLP_GROUP_DIM
GMLP_CHUNK = 128
D_FF = 2816
EPS = 1e-6

LANES = 128
MXU_TILE = 256
DOT_COLS = 2 * MXU_TILE
COL_Q = 0
COL_K = COL_Q + GLA_KEY_WIDTH
COL_V = COL_K + GLA_KEY_WIDTH
COL_G = COL_V + GLA_WIDTH
COL_UV = COL_G + GLA_WIDTH
COL_LR = COL_UV + 2 * GMLP_WIDTH
PROJ_PAD = COL_LR + LANES
QKVG_WIDTH = COL_UV
UVLR_WIDTH = PROJ_PAD - COL_UV

SEQ_TILE = 512
HALF_TILE = SEQ_TILE // 2
N_CHUNKS = SEQ_TILE // GLA_CHUNK
N_GMLP = SEQ_TILE // GMLP_CHUNK
FF_BOUNDS = list(range(0, D_FF, DOT_COLS)) + [D_FF]
VMEM_LIMIT_BYTES = 60 * 1024 * 1024

_NT = (((1,), (1,)), ((), ()))
_TN = (((0,), (0,)), ((), ()))


def _dot(a, b):
    return jnp.dot(a, b, preferred_element_type=F32)


def _rms(x, g):
    return x * lax.rsqrt(jnp.mean(x * x, axis=-1, keepdims=True) + EPS) * g


def _log_sigmoid(z):
    return jnp.minimum(z, 0.0) - jnp.log(1.0 + jnp.exp(-jnp.abs(z)))


def _sigmoid(z):
    return 1.0 / (1.0 + jnp.exp(-z))


def _gelu(x):
    return 0.5 * x * (1.0 + lax.erf(x * (0.5 ** 0.5)))


def _interleave(*streams, windows=None):
    windows = windows or [(0.0, 1.0)] * len(streams)
    totals = [float(sum(c for c, _ in s)) or 1.0 for s in streams]
    pos = [0] * len(streams)
    done = [0.0] * len(streams)

    def when(k):
        lo, hi = windows[k]
        return lo + (hi - lo) * (done[k] + 0.5 * streams[k][pos[k]][0]) / totals[k]

    while True:
        live = [k for k in range(len(streams)) if pos[k] < len(streams[k])]
        if not live:
            return
        k = min(live, key=when)
        cost, thunk = streams[k][pos[k]]
        thunk()
        done[k] += cost
        pos[k] += 1


def _skewed(first, second, order):
    seq = []
    for k, item in enumerate(order):
        seq.append(first(item))
        if k >= 1:
            seq.append(second(order[k - 1]))
    seq.append(second(order[-1]))
    return seq


def _gla_stages(read, lr_read, wdec_ref, bdec_ref, state_ref, b_scr, dec_scr, o_write, reverse):
    last = 0 if reverse else GLA_CHUNK - 1
    keep = {}

    def decay_dot(r0):
        def f():
            keep["z", r0] = _dot(lr_read(slice(r0, r0 + HALF_TILE)), wdec_ref[...])
        return (150, f)

    def decay_cumsum(r0):
        def f():
            la = _log_sigmoid(keep.pop(("z", r0)) + bdec_ref[...]) * (1.0 / GLA_TAU)
            r = lax.broadcasted_iota(jnp.int32, (HALF_TILE, HALF_TILE), 0)
            c = lax.broadcasted_iota(jnp.int32, (HALF_TILE, HALF_TILE), 1)
            same_chunk = (r // GLA_CHUNK) == (c // GLA_CHUNK)
            tri = jnp.where(jnp.logical_and(same_chunk, (c >= r) if reverse else (c <= r)), 1.0, 0.0).astype(BF16)
            hi = la.astype(BF16)
            lo = (la - hi.astype(F32)).astype(BF16)
            both = _dot(tri, jnp.concatenate([hi, lo], axis=1))
            b = both[:, :GLA_KEY_WIDTH] + both[:, GLA_KEY_WIDTH:]
            for p in range(GLA_PAIRS):
                b_scr[p, r0:r0 + HALF_TILE, :] = b[:, p * LANES:(p + 1) * LANES]
        return (400, f)

    def chunk_decays():
        pad = jnp.zeros((LANES - N_CHUNKS, LANES), F32)
        for p in range(GLA_PAIRS):
            b_last = b_scr[p, pl.ds(last, N_CHUNKS, stride=GLA_CHUNK), :]
            dec_scr[p] = jnp.exp(jnp.concatenate([b_last, pad], axis=0).T)

    halves = [HALF_TILE, 0] if reverse else [0, HALF_TILE]
    decay_stages = [decay_dot(r0) for r0 in halves] + [decay_cumsum(r0) for r0 in halves] + [(100, chunk_decays)]

    def first(c):
        def f():
            ci = lax.broadcasted_iota(jnp.int32, (GLA_CHUNK, LANES), 1)
            first_head = ci < GLA_DK
            zeros_k = jnp.zeros((GLA_CHUNK, LANES), F32)
            rows = slice(c * GLA_CHUNK, (c + 1) * GLA_CHUNK)
            b = jnp.concatenate([b_scr[p, rows, :] for p in range(GLA_PAIRS)], axis=1)
            e_last = jnp.exp(b[last:last + 1, :])
            qd = (read(rows, slice(COL_Q, COL_K)) * (jnp.exp(b) * GLA_DK ** -0.5)).astype(BF16)
            kd = read(rows, slice(COL_K, COL_V)) * jnp.exp(-b)
            kte = (kd * e_last).astype(BF16)
            vb = read(rows, slice(COL_V, COL_G)).astype(BF16)
            blocks = []
            for p in range(GLA_PAIRS):
                kd_p = kd[:, p * LANES:(p + 1) * LANES]
                for blk in (jnp.where(first_head, kd_p, 0.0), jnp.where(first_head, 0.0, kd_p)):
                    blocks.append(jnp.concatenate([blk, zeros_k] if p == 0 else [zeros_k, blk], axis=1))
            kd_bd = jnp.concatenate(blocks, axis=0).astype(BF16)
            scores = lax.dot_general(qd, kd_bd, _NT, preferred_element_type=F32)
            d_state = lax.dot_general(kte, vb, _TN, preferred_element_type=F32)
            keep[c] = (qd, vb, scores, d_state)
        return (260, f)

    def second(c):
        def f():
            ri = lax.broadcasted_iota(jnp.int32, (GLA_CHUNK, GLA_KEY_WIDTH), 0)
            ti = lax.broadcasted_iota(jnp.int32, (GLA_CHUNK, GLA_KEY_WIDTH), 1) & (GLA_DK - 1)
            causal = (ti >= ri) if reverse else (ti <= ri)
            zeros_v = jnp.zeros((GLA_CHUNK, GLA_DV), BF16)
            qd, vb, scores, d_state = keep.pop(c)
            a = jnp.where(causal, scores, 0.0).astype(BF16)
            for p in range(GLA_PAIRS):
                h0, h1 = 2 * p, 2 * p + 1
                lanes = slice(p * LANES, (p + 1) * LANES)
                s0 = state_ref[h0 * GLA_DK:(h0 + 1) * GLA_DK, :]
                s1 = state_ref[h1 * GLA_DK:(h1 + 1) * GLA_DK, :]
                rhs = jnp.concatenate([
                    jnp.concatenate([vb[:, h0 * GLA_DV:(h0 + 1) * GLA_DV], zeros_v], axis=1),
                    jnp.concatenate([zeros_v, vb[:, h1 * GLA_DV:(h1 + 1) * GLA_DV]], axis=1),
                    jnp.concatenate([s0.astype(BF16), zeros_v], axis=1),
                    jnp.concatenate([zeros_v, s1.astype(BF16)], axis=1)], axis=0)
                o_write(c, p, _dot(jnp.concatenate([a[:, lanes], qd[:, lanes]], axis=1), rhs))
                dec0 = dec_scr[p, :GLA_DK, c:c + 1]
                dec1 = dec_scr[p, GLA_DK:, c:c + 1]
                state_ref[h0 * GLA_DK:(h0 + 1) * GLA_DK, :] = (
                    s0 * dec0 + d_state[h0 * GLA_DK:(h0 + 1) * GLA_DK, h0 * GLA_DV:(h0 + 1) * GLA_DV])
                state_ref[h1 * GLA_DK:(h1 + 1) * GLA_DK, :] = (
                    s1 * dec1 + d_state[h1 * GLA_DK:(h1 + 1) * GLA_DK, h1 * GLA_DV:(h1 + 1) * GLA_DV])
        return (120, f)

    return decay_stages, first, second


def _fwd_sweep_kernel(x_ref, n1g_ref, wqkvg_ref, wuvlr_ref, wdec_ref, bdec_ref, lng_ref, lnb_ref, wsp_ref, bsp_ref,
                      qkvg_ref, lr_ref, of_ref, yb_ref,
                      p_even, p_odd, h_scr, vn_scr, state_ref, b_scr, dec_scr, *, tiles_per_seq):
    i = pl.program_id(0)

    def project_stages(p_w):
        def norm(r0):
            def f():
                h_scr[r0:r0 + HALF_TILE, :] = _rms(x_ref[r0:r0 + HALF_TILE, :], n1g_ref[...]).astype(BF16)
            return (300, f)

        def piece(r0, w_ref, base, lo, hi):
            def f():
                rows = slice(r0, r0 + HALF_TILE)
                val = _dot(h_scr[rows, :], w_ref[:, lo:hi])
                p_w[rows, base + lo:base + hi] = val
                if w_ref is wqkvg_ref:
                    qkvg_ref[rows, lo:hi] = val.astype(BF16)
                elif base + hi == PROJ_PAD:
                    lr_ref[rows, :] = val[:, COL_LR - base - lo:].astype(BF16)
            return (hi - lo, f)

        stages = [norm(0), norm(HALF_TILE)]
        pieces = [(wqkvg_ref, 0, lo, lo + DOT_COLS) for lo in range(0, QKVG_WIDTH, DOT_COLS)]
        pieces += [(wuvlr_ref, COL_UV, 0, DOT_COLS), (wuvlr_ref, COL_UV, DOT_COLS, UVLR_WIDTH)]
        for w_ref, base, lo, hi in pieces:
            for r0 in (0, HALF_TILE):
                stages.append(piece(r0, w_ref, base, lo, hi))
        return stages

    def mix_stages(p_r):
        def o_write(c, pair, val):
            of_ref[c * GLA_CHUNK:(c + 1) * GLA_CHUNK, pair * 2 * GLA_DV:(pair + 1) * 2 * GLA_DV] = val

        decay_stages, first, second = _gla_stages(
            lambda rows, cols: p_r[rows, cols], lambda rows: p_r[rows, COL_LR:PROJ_PAD].astype(BF16),
            wdec_ref, bdec_ref, state_ref, b_scr, dec_scr, o_write, reverse=False)

        keep = {}

        def gmlp_norm(n):
            def f():
                rows = slice(n * GMLP_CHUNK, (n + 1) * GMLP_CHUNK)
                vg = _gelu(p_r[rows, COL_UV + GMLP_WIDTH:COL_LR])
                mu = jnp.mean(vg, axis=-1, keepdims=True)
                vc = vg - mu
                vn = vc * lax.rsqrt(jnp.mean(vc * vc, axis=-1, keepdims=True) + EPS) * lng_ref[...] + lnb_ref[...]
                for g in range(GMLP_GROUPS):
                    vn_scr[g, :, n * GMLP_CHUNK:(n + 1) * GMLP_CHUNK] = (
                        vn[:, g * GMLP_GROUP_DIM:(g + 1) * GMLP_GROUP_DIM].astype(BF16))
            return (300, f)

        def gmlp_mix():
            for g in range(GMLP_GROUPS):
                keep[g] = _dot(wsp_ref[g], vn_scr[g]) + bsp_ref[g]

        def gmlp_gate(n):
            def f():
                rows = slice(n * GMLP_CHUNK, (n + 1) * GMLP_CHUNK)
                for g in range(GMLP_GROUPS):
                    cols = slice(g * GMLP_GROUP_DIM, (g + 1) * GMLP_GROUP_DIM)
                    u = _gelu(p_r[rows, COL_UV + g * GMLP_GROUP_DIM:COL_UV + (g + 1) * GMLP_GROUP_DIM])
                    yb_ref[rows, cols] = (u * keep[g][:, n * GMLP_CHUNK:(n + 1) * GMLP_CHUNK]).astype(BF16)
            return (200, f)

        chunks = _skewed(first, second, list(range(N_CHUNKS)))
        norms = [gmlp_norm(n) for n in range(N_GMLP)]
        gates = [gmlp_gate(n) for n in range(N_GMLP)]
        stages = list(decay_stages)
        per = (len(chunks) - N_GMLP) // N_GMLP
        for n in range(N_GMLP):
            stages += chunks[n * per:(n + 1) * per] + [norms[n]]
        stages.append((300, gmlp_mix))
        for n, st in enumerate(chunks[N_GMLP * per:]):
            stages += [st, gates[n]]
        return stages

    @pl.when(i == 0)
    def _():
        p_odd[...] = jnp.zeros_like(p_odd)

    @pl.when(jnp.logical_or(i == 0, lax.rem(i - 1, tiles_per_seq) == 0))
    def _():
        state_ref[...] = jnp.zeros_like(state_ref)

    @pl.when(lax.rem(i, 2) == 0)
    def _():
        _interleave(project_stages(p_even), mix_stages(p_odd))

    @pl.when(lax.rem(i, 2) == 1)
    def _():
        _interleave(project_stages(p_odd), mix_stages(p_even))


def _bwd_sweep_kernel(qkvg_ref, lr_ref, of_ref, yb_ref, x_ref, wdec_ref, bdec_ref, gng_ref, wout_ref,
                      n2g_ref, wg_ref, wu_ref, wd_ref, fng_ref,
                      out_ref,
                      x2_even, x2_odd, h2_scr, a_scr, acc_scr, o_scr, y_scr, state_ref, b_scr, dec_scr,
                      *, tiles_per_seq, final_norm):
    j = pl.program_id(0)

    def mix_stages(x2_w):
        def o_write(c, pair, val):
            o_scr[c * GLA_CHUNK:(c + 1) * GLA_CHUNK, pair * 2 * GLA_DV:(pair + 1) * 2 * GLA_DV] = val

        decay_stages, first, second = _gla_stages(
            lambda rows, cols: qkvg_ref[rows, cols].astype(F32), lambda rows: lr_ref[rows, :],
            wdec_ref, bdec_ref, state_ref, b_scr, dec_scr, o_write, reverse=True)

        def gate(r0):
            def f():
                rows = slice(r0, r0 + HALF_TILE)
                g = qkvg_ref[rows, COL_G:COL_UV].astype(F32)
                g = g * _sigmoid(g)
                for hd in range(GLA_HEADS):
                    cols = slice(hd * GLA_DV, (hd + 1) * GLA_DV)
                    o = of_ref[rows, cols] + o_scr[rows, cols]
                    o = o * lax.rsqrt(jnp.mean(o * o, axis=-1, keepdims=True) + EPS)
                    y_scr[rows, cols] = (o * gng_ref[:, cols] * g[:, cols]).astype(BF16)
                y_scr[rows, GLA_WIDTH:] = yb_ref[rows, :]
            return (250, f)

        def out_proj(r0, lo):
            def f():
                rows = slice(r0, r0 + HALF_TILE)
                x2_w[rows, lo:lo + DOT_COLS] = (x_ref[rows, lo:lo + DOT_COLS]
                                                + _dot(y_scr[rows, :], wout_ref[:, lo:lo + DOT_COLS]))
            return (512, f)

        chunks = _skewed(first, second, list(range(N_CHUNKS - 1, -1, -1)))
        per_half = len(chunks) // 2
        upper = [gate(HALF_TILE)] + [out_proj(HALF_TILE, lo) for lo in range(0, D_MODEL, DOT_COLS)]
        lower = [gate(0)] + [out_proj(0, lo) for lo in range(0, D_MODEL, DOT_COLS)]
        tail = list(chunks[per_half + 1:])
        stages = decay_stages + chunks[:per_half + 1]
        for k, st in enumerate(upper):
            stages.append(st)
            stages += tail[2 * k:2 * k + 2]
        stages += tail[2 * len(upper):]
        return stages + lower

    def ffn_stages(x2_r):
        def norm(r0):
            def f():
                h2_scr[r0:r0 + HALF_TILE, :] = _rms(x2_r[r0:r0 + HALF_TILE, :], n2g_ref[...]).astype(BF16)
            return (300, f)

        def up(c):
            lo, hi = FF_BOUNDS[c], FF_BOUNDS[c + 1]

            def f():
                h2 = h2_scr[...]
                g = _dot(h2, wg_ref[:, lo:hi])
                u = _dot(h2, wu_ref[:, lo:hi])
                a_scr[c % 2, :, :hi - lo] = (g * _sigmoid(g) * u).astype(BF16)
            return (4 * (hi - lo), f)

        def down(c):
            lo, hi = FF_BOUNDS[c], FF_BOUNDS[c + 1]

            def f():
                d = _dot(a_scr[c % 2, :, :hi - lo], wd_ref[lo:hi, :])
                if c == 0:
                    acc_scr[...] = d
                else:
                    acc_scr[...] += d
            return (2 * (hi - lo), f)

        def final(r0):
            def f():
                rows = slice(r0, r0 + HALF_TILE)
                y = x2_r[rows, :] + acc_scr[rows, :]
                if final_norm:
                    y = _rms(y, fng_ref[...])
                out_ref[rows, :] = y
            return (300, f)

        return ([norm(0), norm(HALF_TILE)] + _skewed(up, down, list(range(len(FF_BOUNDS) - 1)))
                + [final(0), final(HALF_TILE)])

    @pl.when(j == 0)
    def _():
        x2_odd[...] = jnp.zeros_like(x2_odd)

    @pl.when(lax.rem(j, tiles_per_seq) == 0)
    def _():
        state_ref[...] = jnp.zeros_like(state_ref)

    @pl.when(lax.rem(j, 2) == 0)
    def _():
        _interleave(ffn_stages(x2_odd), mix_stages(x2_even))

    @pl.when(lax.rem(j, 2) == 1)
    def _():
        _interleave(ffn_stages(x2_even), mix_stages(x2_odd))


def _const_spec(shape):
    return pl.BlockSpec(shape, lambda *_: (0,) * len(shape), pipeline_mode=pl.Buffered(1))


def _params():
    return pltpu.CompilerParams(dimension_semantics=("arbitrary",), vmem_limit_bytes=VMEM_LIMIT_BYTES)


def _layer(x, n1g, w_in, wdf, bdf, wdb, bdb, gng, lng, lnb, wsp, bsp, w_out, n2g, wg, wu, wd, fng, final_norm):
    B, S, D = x.shape
    ts = SEQ_TILE
    assert S % ts == 0 and D == D_MODEL
    nt = S // ts
    n = B * nt
    x2d = x.reshape(B * S, D)

    lr0 = COL_UV
    w_qkvg = w_in[:, :lr0].astype(BF16)
    w_uvlr = jnp.concatenate([w_in[:, lr0 + 2 * GLA_LOWRANK:], w_in[:, lr0:lr0 + 2 * GLA_LOWRANK],
                              jnp.zeros((D, LANES - 2 * GLA_LOWRANK), w_in.dtype)], axis=1).astype(BF16)
    zpad = jnp.zeros((LANES - 2 * GLA_LOWRANK, GLA_KEY_WIDTH), F32)
    zlr = jnp.zeros((GLA_LOWRANK, GLA_KEY_WIDTH), F32)
    wdf_p = jnp.concatenate([wdf, zlr, zpad], axis=0).astype(BF16)
    wdb_p = jnp.concatenate([zlr, wdb, zpad], axis=0).astype(BF16)

    cur = lambda w: pl.BlockSpec((ts, w), lambda i: (jnp.minimum(i, n - 1), 0))
    prev = lambda w: pl.BlockSpec((ts, w), lambda i: (jnp.maximum(i - 1, 0), 0))
    gla_scratch = [pltpu.VMEM((GLA_KEY_WIDTH, GLA_DV), F32), pltpu.VMEM((GLA_PAIRS, ts, LANES), F32),
                   pltpu.VMEM((GLA_PAIRS, LANES, LANES), F32)]
    qkvg, lr, o_f, y_b = pl.pallas_call(
        functools.partial(_fwd_sweep_kernel, tiles_per_seq=nt),
        grid=(n + 1,),
        in_specs=[cur(D), _const_spec((1, D)), _const_spec((D, QKVG_WIDTH)), _const_spec((D, UVLR_WIDTH)),
                  _const_spec((LANES, GLA_KEY_WIDTH)),
                  _const_spec((1, GLA_KEY_WIDTH)), _const_spec((1, GMLP_WIDTH)), _const_spec((1, GMLP_WIDTH)),
                  _const_spec((GMLP_GROUPS, GMLP_CHUNK, GMLP_CHUNK)), _const_spec((GMLP_GROUPS, GMLP_CHUNK, 1))],
        out_specs=[cur(QKVG_WIDTH), cur(LANES), prev(GLA_WIDTH), prev(GMLP_WIDTH)],
        out_shape=[jax.ShapeDtypeStruct((B * S, QKVG_WIDTH), BF16), jax.ShapeDtypeStruct((B * S, LANES), BF16),
                   jax.ShapeDtypeStruct((B * S, GLA_WIDTH), F32), jax.ShapeDtypeStruct((B * S, GMLP_WIDTH), BF16)],
        scratch_shapes=[pltpu.VMEM((ts, PROJ_PAD), F32), pltpu.VMEM((ts, PROJ_PAD), F32),
                        pltpu.VMEM((ts, D), BF16), pltpu.VMEM((GMLP_GROUPS, GMLP_CHUNK, ts), BF16)] + gla_scratch,
        compiler_params=_params(),
        name="fwd_sweep",
    )(x2d, n1g.reshape(1, D), w_qkvg, w_uvlr, wdf_p, bdf.reshape(1, -1), lng.reshape(1, -1), lnb.reshape(1, -1),
      wsp.astype(BF16), bsp.reshape(GMLP_GROUPS, GMLP_CHUNK, 1))

    def rblock(j):
        j = jnp.clip(j, 0, n - 1)
        return (j // nt) * nt + (nt - 1 - j % nt)

    rcur = lambda w: pl.BlockSpec((ts, w), lambda j: (rblock(j), 0))
    rprev = lambda w: pl.BlockSpec((ts, w), lambda j: (rblock(j - 1), 0))
    out = pl.pallas_call(
        functools.partial(_bwd_sweep_kernel, tiles_per_seq=nt, final_norm=final_norm),
        grid=(n + 1,),
        in_specs=[rcur(QKVG_WIDTH), rcur(LANES), rcur(GLA_WIDTH), rcur(GMLP_WIDTH), rcur(D),
                  _const_spec((LANES, GLA_KEY_WIDTH)), _const_spec((1, GLA_KEY_WIDTH)),
                  _const_spec((1, GLA_WIDTH)), _const_spec((D, D)),
                  _const_spec((1, D)), _const_spec((D, D_FF)), _const_spec((D, D_FF)), _const_spec((D_FF, D)),
                  _const_spec((1, D))],
        out_specs=rprev(D),
        out_shape=jax.ShapeDtypeStruct((B * S, D), F32),
        scratch_shapes=[pltpu.VMEM((ts, D), F32), pltpu.VMEM((ts, D), F32), pltpu.VMEM((ts, D), BF16),
                        pltpu.VMEM((2, ts, DOT_COLS), BF16), pltpu.VMEM((ts, D), F32),
                        pltpu.VMEM((ts, GLA_WIDTH), F32), pltpu.VMEM((ts, D), BF16)] + gla_scratch,
        compiler_params=_params(),
        name="bwd_sweep",
    )(qkvg, lr, o_f, y_b, x2d, wdb_p, bdb.reshape(1, -1), gng.reshape(1, -1), w_out.astype(BF16),
      n2g.reshape(1, D), wg.astype(BF16), wu.astype(BF16), wd.astype(BF16), fng.reshape(1, D))
    return out.reshape(B, S, D)


def kernel(x, norm1_g, w_in, w_decay_f, b_decay_f, w_decay_b, b_decay_b, gla_norm_g, gmlp_ln_g, gmlp_ln_b,
           w_spatial, b_spatial, w_out, norm2_g, w_gate, w_up, w_down, final_norm_g):
    depth = norm1_g.shape[0]
    for l in range(depth):
        x = _layer(x, norm1_g[l], w_in[l], w_decay_f[l], b_decay_f[l], w_decay_b[l], b_decay_b[l],
                   gla_norm_g[l], gmlp_ln_g[l], gmlp_ln_b[l], w_spatial[l], b_spatial[l], w_out[l],
                   norm2_g[l], w_gate[l], w_up[l], w_down[l], final_norm_g, final_norm=(l == depth - 1))
    return x
```

```python
import functools

import jax
import jax.numpy as jnp
from jax import lax
from jax.experimental import pallas as pl
from jax.experimental.pallas import tpu as pltpu

F32 = jnp.float32
BF16 = jnp.bfloat16

D_MODEL = 1024
GLA_HEADS = 4
GLA_PAIRS = GLA_HEADS // 2
GLA_DK = 64
GLA_DV = 128
GLA_KEY_WIDTH = GLA_HEADS * GLA_DK
GLA_WIDTH = GLA_HEADS * GLA_DV
GLA_LOWRANK = 16
GLA_TAU = 16.0
GLA_CHUNK = 64
GMLP_GROUPS = 4
GMLP_GROUP_DIM = 128
GMLP_WIDTH = GMLP_GROUPS * GMLP_GROUP_DIM
GMLP_CHUNK = 128
D_FF = 2816
EPS = 1e-6

LANES = 128
MXU_TILE = 256
DOT_COLS = 2 * MXU_TILE
COL_Q = 0
COL_K = COL_Q + GLA_KEY_WIDTH
COL_V = COL_K + GLA_KEY_WIDTH
COL_G = COL_V + GLA_WIDTH
COL_UV = COL_G + GLA_WIDTH
COL_LR = COL_UV + 2 * GMLP_WIDTH
PROJ_PAD = COL_LR + LANES
QKVG_WIDTH = COL_UV
UVLR_WIDTH = PROJ_PAD - COL_UV

SEQ_TILE = 512
HALF_TILE = SEQ_TILE // 2
N_CHUNKS = SEQ_TILE // GLA_CHUNK
N_GMLP = SEQ_TILE // GMLP_CHUNK
FF_BOUNDS = list(range(0, D_FF, DOT_COLS)) + [D_FF]
VMEM_LIMIT_BYTES = 60 * 1024 * 1024

_NT = (((1,), (1,)), ((), ()))
_TN = (((0,), (0,)), ((), ()))


def _dot(a, b):
    return jnp.dot(a, b, preferred_element_type=F32)


def _rms(x, g):
    return x * lax.rsqrt(jnp.mean(x * x, axis=-1, keepdims=True) + EPS) * g


def _log_sigmoid(z):
    return jnp.minimum(z, 0.0) - jnp.log(1.0 + jnp.exp(-jnp.abs(z)))


def _sigmoid(z):
    return 1.0 / (1.0 + jnp.exp(-z))


def _gelu(x):
    return 0.5 * x * (1.0 + lax.erf(x * (0.5 ** 0.5)))


def _interleave(*streams, windows=None):
    windows = windows or [(0.0, 1.0)] * len(streams)
    totals = [float(sum(c for c, _ in s)) or 1.0 for s in streams]
    pos = [0] * len(streams)
    done = [0.0] * len(streams)

    def when(k):
        lo, hi = windows[k]
        return lo + (hi - lo) * (done[k] + 0.5 * streams[k][pos[k]][0]) / totals[k]

    while True:
        live = [k for k in range(len(streams)) if pos[k] < len(streams[k])]
        if not live:
            return
        k = min(live, key=when)
        cost, thunk = streams[k][pos[k]]
        thunk()
        done[k] += cost
        pos[k] += 1


def _skewed(first, second, order):
    seq = []
    for k, item in enumerate(order):
        seq.append(first(item))
        if k >= 1:
            seq.append(second(order[k - 1]))
    seq.append(second(order[-1]))
    return seq


def _gla_stages(read, lr_read, wdec_ref, bdec_ref, state_ref, b_scr, dec_scr, o_write, reverse):
    last = 0 if reverse else GLA_CHUNK - 1
    keep = {}

    def decay_dot(r0):
        def f():
            keep["z", r0] = _dot(lr_read(slice(r0, r0 + HALF_TILE)), wdec_ref[...])
        return (150, f)

    def decay_cumsum(r0):
        def f():
            la = _log_sigmoid(keep.pop(("z", r0)) + bdec_ref[...]) * (1.0 / GLA_TAU)
            r = lax.broadcasted_iota(jnp.int32, (HALF_TILE, HALF_TILE), 0)
            c = lax.broadcasted_iota(jnp.int32, (HALF_TILE, HALF_TILE), 1)
            same_chunk = (r // GLA_CHUNK) == (c // GLA_CHUNK)
            tri = jnp.where(jnp.logical_and(same_chunk, (c >= r) if reverse else (c <= r)), 1.0, 0.0).astype(BF16)
            hi = la.astype(BF16)
            lo = (la - hi.astype(F32)).astype(BF16)
            both = _dot(tri, jnp.concatenate([hi, lo], axis=1))
            b = both[:, :GLA_KEY_WIDTH] + both[:, GLA_KEY_WIDTH:]
            for p in range(GLA_PAIRS):
                b_scr[p, r0:r0 + HALF_TILE, :] = b[:, p * LANES:(p + 1) * LANES]
        return (400, f)

    def chunk_decays():
        pad = jnp.zeros((LANES - N_CHUNKS, LANES), F32)
        for p in range(GLA_PAIRS):
            b_last = b_scr[p, pl.ds(last, N_CHUNKS, stride=GLA_CHUNK), :]
            dec_scr[p] = jnp.exp(jnp.concatenate([b_last, pad], axis=0).T)

    halves = [HALF_TILE, 0] if reverse else [0, HALF_TILE]
    decay_stages = [decay_dot(r0) for r0 in halves] + [decay_cumsum(r0) for r0 in halves] + [(100, chunk_decays)]

    def first(c):
        def f():
            ci = lax.broadcasted_iota(jnp.int32, (GLA_CHUNK, LANES), 1)
            first_head = ci < GLA_DK
            zeros_k = jnp.zeros((GLA_CHUNK, LANES), F32)
            rows = slice(c * GLA_CHUNK, (c + 1) * GLA_CHUNK)
            b = jnp.concatenate([b_scr[p, rows, :] for p in range(GLA_PAIRS)], axis=1)
            e_last = jnp.exp(b[last:last + 1, :])
            qd = (read(rows, slice(COL_Q, COL_K)) * (jnp.exp(b) * GLA_DK ** -0.5)).astype(BF16)
            kd = read(rows, slice(COL_K, COL_V)) * jnp.exp(-b)
            kte = (kd * e_last).astype(BF16)
            vb = read(rows, slice(COL_V, COL_G)).astype(BF16)
            blocks = []
            for p in range(GLA_PAIRS):
                kd_p = kd[:, p * LANES:(p + 1) * LANES]
                for blk in (jnp.where(first_head, kd_p, 0.0), jnp.where(first_head, 0.0, kd_p)):
                    blocks.append(jnp.concatenate([blk, zeros_k] if p == 0 else [zeros_k, blk], axis=1))
            kd_bd = jnp.concatenate(blocks, axis=0).astype(BF16)
            scores = lax.dot_general(qd, kd_bd, _NT, preferred_element_type=F32)
            d_state = lax.dot_general(kte, vb, _TN, preferred_element_type=F32)
            keep[c] = (qd, vb, scores, d_state)
        return (260, f)

    def second(c):
        def f():
            ri = lax.broadcasted_iota(jnp.int32, (GLA_CHUNK, GLA_KEY_WIDTH), 0)
            ti = lax.broadcasted_iota(jnp.int32, (GLA_CHUNK, GLA_KEY_WIDTH), 1) & (GLA_DK - 1)
            causal = (ti >= ri) if reverse else (ti <= ri)
            zeros_v = jnp.zeros((GLA_CHUNK, GLA_DV), BF16)
            qd, vb, scores, d_state = keep.pop(c)
            a = jnp.where(causal, scores, 0.0).astype(BF16)
            for p in range(GLA_PAIRS):
                h0, h1 = 2 * p, 2 * p + 1
                lanes = slice(p * LANES, (p + 1) * LANES)
                s0 = state_ref[h0 * GLA_DK:(h0 + 1) * GLA_DK, :]
                s1 = state_ref[h1 * GLA_DK:(h1 + 1) * GLA_DK, :]
                rhs = jnp.concatenate([
                    jnp.concatenate([vb[:, h0 * GLA_DV:(h0 + 1) * GLA_DV], zeros_v], axis=1),
                    jnp.concatenate([zeros_v, vb[:, h1 * GLA_DV:(h1 + 1) * GLA_DV]], axis=1),
                    jnp.concatenate([s0.astype(BF16), zeros_v], axis=1),
                    jnp.concatenate([zeros_v, s1.astype(BF16)], axis=1)], axis=0)
                o_write(c, p, _dot(jnp.concatenate([a[:, lanes], qd[:, lanes]], axis=1), rhs))
                dec0 = dec_scr[p, :GLA_DK, c:c + 1]
                dec1 = dec_scr[p, GLA_DK:, c:c + 1]
                state_ref[h0 * GLA_DK:(h0 + 1) * GLA_DK, :] = (
                    s0 * dec0 + d_state[h0 * GLA_DK:(h0 + 1) * GLA_DK, h0 * GLA_DV:(h0 + 1) * GLA_DV])
                state_ref[h1 * GLA_DK:(h1 + 1) * GLA_DK, :] = (
                    s1 * dec1 + d_state[h1 * GLA_DK:(h1 + 1) * GLA_DK, h1 * GLA_DV:(h1 + 1) * GLA_DV])
        return (120, f)

    return decay_stages, first, second


def _fwd_sweep_kernel(x_ref, n1g_ref, wqkvg_ref, wuvlr_ref, wdec_ref, bdec_ref, lng_ref, lnb_ref, wsp_ref, bsp_ref,
                      wg32_ref, wu32_ref, wd32_ref, wo32_ref,
                      qkvg_ref, lr_ref, of_ref, yb_ref, wg16_ref, wu16_ref, wd16_ref, wo16_ref,
                      p_even, p_odd, h_scr, vn_scr, state_ref, b_scr, dec_scr, *, n_tiles, tiles_per_seq):
    i = pl.program_id(0)

    def cast_stages():
        def cast(src, dst):
            def f():
                dst[...] = src[...].astype(BF16)
            return (40, f)

        return [cast(wg32_ref, wg16_ref), cast(wu32_ref, wu16_ref), cast(wd32_ref, wd16_ref),
                cast(wo32_ref, wo16_ref)]

    def project_stages(p_w):
        def norm(r0):
            def f():
                h_scr[r0:r0 + HALF_TILE, :] = _rms(x_ref[r0:r0 + HALF_TILE, :], n1g_ref[...]).astype(BF16)
            return (300, f)

        def piece(r0, w_ref, base, lo, hi):
            def f():
                rows = slice(r0, r0 + HALF_TILE)
                val = _dot(h_scr[rows, :], w_ref[:, lo:hi])
                p_w[rows, base + lo:base + hi] = val
                if w_ref is wqkvg_ref:
                    qkvg_ref[rows, lo:hi] = val.astype(BF16)
                elif base + hi == PROJ_PAD:
                    lr_ref[rows, :] = val[:, COL_LR - base - lo:].astype(BF16)
            return (hi - lo, f)

        stages = [norm(0), norm(HALF_TILE)]
        pieces = [(wqkvg_ref, 0, lo, lo + DOT_COLS) for lo in range(0, QKVG_WIDTH, DOT_COLS)]
        pieces += [(wuvlr_ref, COL_UV, 0, DOT_COLS), (wuvlr_ref, COL_UV, DOT_COLS, UVLR_WIDTH)]
        for w_ref, base, lo, hi in pieces:
            for r0 in (0, HALF_TILE):
                stages.append(piece(r0, w_ref, base, lo, hi))
        return stages

    def mix_stages(p_r):
        def o_write(c, pair, val):
            of_ref[c * GLA_CHUNK:(c + 1) * GLA_CHUNK, pair * 2 * GLA_DV:(pair + 1) * 2 * GLA_DV] = val

        decay_stages, first, second = _gla_stages(
            lambda rows, cols: p_r[rows, cols], lambda rows: p_r[rows, COL_LR:PROJ_PAD].astype(BF16),
            wdec_ref, bdec_ref, state_ref, b_scr, dec_scr, o_write, reverse=False)

        keep = {}

        def gmlp_norm(n):
            def f():
                rows = slice(n * GMLP_CHUNK, (n + 1) * GMLP_CHUNK)
                vg = _gelu(p_r[rows, COL_UV + GMLP_WIDTH:COL_LR])
                mu = jnp.mean(vg, axis=-1, keepdims=True)
                vc = vg - mu
                vn = vc * lax.rsqrt(jnp.mean(vc * vc, axis=-1, keepdims=True) + EPS) * lng_ref[...] + lnb_ref[...]
                for g in range(GMLP_GROUPS):
                    vn_scr[g, :, n * GMLP_CHUNK:(n + 1) * GMLP_CHUNK] = (
                        vn[:, g * GMLP_GROUP_DIM:(g + 1) * GMLP_GROUP_DIM].astype(BF16))
            return (300, f)

        def gmlp_mix():
            for g in range(GMLP_GROUPS):
                keep[g] = _dot(wsp_ref[g], vn_scr[g]) + bsp_ref[g]

        def gmlp_gate(n):
            def f():
                rows = slice(n * GMLP_CHUNK, (n + 1) * GMLP_CHUNK)
                for g in range(GMLP_GROUPS):
                    cols = slice(g * GMLP_GROUP_DIM, (g + 1) * GMLP_GROUP_DIM)
                    u = _gelu(p_r[rows, COL_UV + g * GMLP_GROUP_DIM:COL_UV + (g + 1) * GMLP_GROUP_DIM])
                    yb_ref[rows, cols] = (u * keep[g][:, n * GMLP_CHUNK:(n + 1) * GMLP_CHUNK]).astype(BF16)
            return (200, f)

        chunks = _skewed(first, second, list(range(N_CHUNKS)))
        norms = [gmlp_norm(n) for n in range(N_GMLP)]
        gates = [gmlp_gate(n) for n in range(N_GMLP)]
        stages = list(decay_stages)
        per = (len(chunks) - N_GMLP) // N_GMLP
        for n in range(N_GMLP):
            stages += chunks[n * per:(n + 1) * per] + [norms[n]]
        stages.append((300, gmlp_mix))
        for n, st in enumerate(chunks[N_GMLP * per:]):
            stages += [st, gates[n]]
        return stages

    @pl.when(lax.rem(i - 1, tiles_per_seq) == 0)
    def _():
        state_ref[...] = jnp.zeros_like(state_ref)

    steady = jnp.logical_and(i > 0, i < n_tiles)

    @pl.when(i == 0)
    def _():
        _interleave(project_stages(p_even), cast_stages())

    @pl.when(jnp.logical_and(steady, lax.rem(i, 2) == 0))
    def _():
        _interleave(project_stages(p_even), mix_stages(p_odd), cast_stages())

    @pl.when(jnp.logical_and(steady, lax.rem(i, 2) == 1))
    def _():
        _interleave(project_stages(p_odd), mix_stages(p_even), cast_stages())

    @pl.when(i == n_tiles)
    def _():
        _interleave(mix_stages(p_even if (n_tiles - 1) % 2 == 0 else p_odd), cast_stages())


def _bwd_sweep_kernel(qkvg_ref, lr_ref, of_ref, yb_ref, x_ref, wdec_ref, bdec_ref, gng_ref, wout_ref,
                      n2g_ref, wg_ref, wu_ref, wd_ref, fng_ref,
                      out_ref,
                      x2_even, x2_odd, h2_scr, a_scr, acc_scr, o_scr, y_scr, state_ref, b_scr, dec_scr,
                      *, n_tiles, tiles_per_seq, final_norm):
    j = pl.program_id(0)

    def mix_stages(x2_w):
        def o_write(c, pair, val):
            o_scr[c * GLA_CHUNK:(c + 1) * GLA_CHUNK, pair * 2 * GLA_DV:(pair + 1) * 2 * GLA_DV] = val

        decay_stages, first, second = _gla_stages(
            lambda rows, cols: qkvg_ref[rows, cols].astype(F32), lambda rows: lr_ref[rows, :],
            wdec_ref, bdec_ref, state_ref, b_scr, dec_scr, o_write, reverse=True)

        def gate(r0):
            def f():
                rows = slice(r0, r0 + HALF_TILE)
                g = qkvg_ref[rows, COL_G:COL_UV].astype(F32)
                g = g * _sigmoid(g)
                for hd in range(GLA_HEADS):
                    cols = slice(hd * GLA_DV, (hd + 1) * GLA_DV)
                    o = of_ref[rows, cols] + o_scr[rows, cols]
                    o = o * lax.rsqrt(jnp.mean(o * o, axis=-1, keepdims=True) + EPS)
                    y_scr[rows, cols] = (o * gng_ref[:, cols] * g[:, cols]).astype(BF16)
                y_scr[rows, GLA_WIDTH:] = yb_ref[rows, :]
            return (250, f)

        def out_proj(r0, lo):
            def f():
                rows = slice(r0, r0 + HALF_TILE)
                x2_w[rows, lo:lo + DOT_COLS] = (x_ref[rows, lo:lo + DOT_COLS]
                                                + _dot(y_scr[rows, :], wout_ref[:, lo:lo + DOT_COLS]))
            return (512, f)

        chunks = _skewed(first, second, list(range(N_CHUNKS - 1, -1, -1)))
        per_half = len(chunks) // 2
        upper = [gate(HALF_TILE)] + [out_proj(HALF_TILE, lo) for lo in range(0, D_MODEL, DOT_COLS)]
        lower = [gate(0)] + [out_proj(0, lo) for lo in range(0, D_MODEL, DOT_COLS)]
        tail = list(chunks[per_half + 1:])
        stages = decay_stages + chunks[:per_half + 1]
        for k, st in enumerate(upper):
            stages.append(st)
            stages += tail[2 * k:2 * k + 2]
        stages += tail[2 * len(upper):]
        return stages + lower

    def ffn_stages(x2_r):
        def norm(r0):
            def f():
                h2_scr[r0:r0 + HALF_TILE, :] = _rms(x2_r[r0:r0 + HALF_TILE, :], n2g_ref[...]).astype(BF16)
            return (300, f)

        def up(c):
            lo, hi = FF_BOUNDS[c], FF_BOUNDS[c + 1]

            def f():
                h2 = h2_scr[...]
                g = _dot(h2, wg_ref[:, lo:hi])
                u = _dot(h2, wu_ref[:, lo:hi])
                a_scr[c % 2, :, :hi - lo] = (g * _sigmoid(g) * u).astype(BF16)
            return (4 * (hi - lo), f)

        def down(c):
            lo, hi = FF_BOUNDS[c], FF_BOUNDS[c + 1]

            def f():
                d = _dot(a_scr[c % 2, :, :hi - lo], wd_ref[lo:hi, :])
                if c == 0:
                    acc_scr[...] = d
                else:
                    acc_scr[...] += d
            return (2 * (hi - lo), f)

        def final(r0):
            def f():
                rows = slice(r0, r0 + HALF_TILE)
                y = x2_r[rows, :] + acc_scr[rows, :]
                if final_norm:
                    y = _rms(y, fng_ref[...])
                out_ref[rows, :] = y
            return (300, f)

        return ([norm(0), norm(HALF_TILE)] + _skewed(up, down, list(range(len(FF_BOUNDS) - 1)))
                + [final(0), final(HALF_TILE)])

    @pl.when(lax.rem(j, tiles_per_seq) == 0)
    def _():
        state_ref[...] = jnp.zeros_like(state_ref)

    steady = jnp.logical_and(j > 0, j < n_tiles)

    @pl.when(j == 0)
    def _():
        _interleave(mix_stages(x2_even))

    @pl.when(jnp.logical_and(steady, lax.rem(j, 2) == 0))
    def _():
        _interleave(ffn_stages(x2_odd), mix_stages(x2_even))

    @pl.when(jnp.logical_and(steady, lax.rem(j, 2) == 1))
    def _():
        _interleave(ffn_stages(x2_even), mix_stages(x2_odd))

    @pl.when(j == n_tiles)
    def _():
        _interleave(ffn_stages(x2_even if (n_tiles - 1) % 2 == 0 else x2_odd))


def _const_spec(shape):
    return pl.BlockSpec(shape, lambda *_: (0,) * len(shape), pipeline_mode=pl.Buffered(1))


def _params():
    return pltpu.CompilerParams(dimension_semantics=("arbitrary",), vmem_limit_bytes=VMEM_LIMIT_BYTES)


def _layer(x, n1g, w_in, wdf, bdf, wdb, bdb, gng, lng, lnb, wsp, bsp, w_out, n2g, wg, wu, wd, fng, final_norm):
    B, S, D = x.shape
    ts = SEQ_TILE
    assert S % ts == 0 and D == D_MODEL
    nt = S // ts
    n = B * nt
    x2d = x.reshape(B * S, D)

    lr0 = COL_UV
    w_qkvg = w_in[:, :lr0].astype(BF16)
    w_uvlr = jnp.concatenate([w_in[:, lr0 + 2 * GLA_LOWRANK:], w_in[:, lr0:lr0 + 2 * GLA_LOWRANK],
                              jnp.zeros((D, LANES - 2 * GLA_LOWRANK), w_in.dtype)], axis=1).astype(BF16)
    zpad = jnp.zeros((LANES - 2 * GLA_LOWRANK, GLA_KEY_WIDTH), F32)
    zlr = jnp.zeros((GLA_LOWRANK, GLA_KEY_WIDTH), F32)
    wdf_p = jnp.concatenate([wdf, zlr, zpad], axis=0).astype(BF16)
    wdb_p = jnp.concatenate([zlr, wdb, zpad], axis=0).astype(BF16)

    cur = lambda w: pl.BlockSpec((ts, w), lambda i: (jnp.minimum(i, n - 1), 0))
    prev = lambda w: pl.BlockSpec((ts, w), lambda i: (jnp.maximum(i - 1, 0), 0))

    def rows_spec(w):
        rows, cols = w.shape
        blk = next(r for r in range(16, rows + 1, 16) if rows % r == 0 and r * (n + 1) >= rows)
        return pl.BlockSpec((blk, cols), lambda i: (jnp.minimum(i, rows // blk - 1), 0))

    ffn_weights = [wg, wu, wd, w_out]
    gla_scratch = [pltpu.VMEM((GLA_KEY_WIDTH, GLA_DV), F32), pltpu.VMEM((GLA_PAIRS, ts, LANES), F32),
                   pltpu.VMEM((GLA_PAIRS, LANES, LANES), F32)]
    qkvg, lr, o_f, y_b, wg_b, wu_b, wd_b, wo_b = pl.pallas_call(
        functools.partial(_fwd_sweep_kernel, n_tiles=n, tiles_per_seq=nt),
        grid=(n + 1,),
        in_specs=[cur(D), _const_spec((1, D)), _const_spec((D, QKVG_WIDTH)), _const_spec((D, UVLR_WIDTH)),
                  _const_spec((LANES, GLA_KEY_WIDTH)),
                  _const_spec((1, GLA_KEY_WIDTH)), _const_spec((1, GMLP_WIDTH)), _const_spec((1, GMLP_WIDTH)),
                  _const_spec((GMLP_GROUPS, GMLP_CHUNK, GMLP_CHUNK)), _const_spec((GMLP_GROUPS, GMLP_CHUNK, 1))]
                 + [rows_spec(w) for w in ffn_weights],
        out_specs=[cur(QKVG_WIDTH), cur(LANES), prev(GLA_WIDTH), prev(GMLP_WIDTH)]
                  + [rows_spec(w) for w in ffn_weights],
        out_shape=[jax.ShapeDtypeStruct((B * S, QKVG_WIDTH), BF16), jax.ShapeDtypeStruct((B * S, LANES), BF16),
                   jax.ShapeDtypeStruct((B * S, GLA_WIDTH), F32), jax.ShapeDtypeStruct((B * S, GMLP_WIDTH), BF16)]
                  + [jax.ShapeDtypeStruct(w.shape, BF16) for w in ffn_weights],
        scratch_shapes=[pltpu.VMEM((ts, PROJ_PAD), F32), pltpu.VMEM((ts, PROJ_PAD), F32),
                        pltpu.VMEM((ts, D), BF16), pltpu.VMEM((GMLP_GROUPS, GMLP_CHUNK, ts), BF16)] + gla_scratch,
        compiler_params=_params(),
        name="fwd_sweep",
    )(x2d, n1g.reshape(1, D), w_qkvg, w_uvlr, wdf_p, bdf.reshape(1, -1), lng.reshape(1, -1), lnb.reshape(1, -1),
      wsp.astype(BF16), bsp.reshape(GMLP_GROUPS, GMLP_CHUNK, 1), *ffn_weights)

    def rblock(j):
        j = jnp.clip(j, 0, n - 1)
        return (j // nt) * nt + (nt - 1 - j % nt)

    rcur = lambda w: pl.BlockSpec((ts, w), lambda j: (rblock(j), 0))
    rprev = lambda w: pl.BlockSpec((ts, w), lambda j: (rblock(j - 1), 0))
    out = pl.pallas_call(
        functools.partial(_bwd_sweep_kernel, n_tiles=n, tiles_per_seq=nt, final_norm=final_norm),
        grid=(n + 1,),
        in_specs=[rcur(QKVG_WIDTH), rcur(LANES), rcur(GLA_WIDTH), rcur(GMLP_WIDTH), rcur(D),
                  _const_spec((LANES, GLA_KEY_WIDTH)), _const_spec((1, GLA_KEY_WIDTH)),
                  _const_spec((1, GLA_WIDTH)), _const_spec((D, D)),
                  _const_spec((1, D)), _const_spec((D, D_FF)), _const_spec((D, D_FF)), _const_spec((D_FF, D)),
                  _const_spec((1, D))],
        out_specs=rprev(D),
        out_shape=jax.ShapeDtypeStruct((B * S, D), F32),
        scratch_shapes=[pltpu.VMEM((ts, D), F32), pltpu.VMEM((ts, D), F32), pltpu.VMEM((ts, D), BF16),
                        pltpu.VMEM((2, ts, DOT_COLS), BF16), pltpu.VMEM((ts, D), F32),
                        pltpu.VMEM((ts, GLA_WIDTH), F32), pltpu.VMEM((ts, D), BF16)] + gla_scratch,
        compiler_params=_params(),
        name="bwd_sweep",
    )(qkvg, lr, o_f, y_b, x2d, wdb_p, bdb.reshape(1, -1), gng.reshape(1, -1), wo_b,
      n2g.reshape(1, D), wg_b, wu_b, wd_b, fng.reshape(1, D))
    return out.reshape(B, S, D)


def kernel(x, norm1_g, w_in, w_decay_f, b_decay_f, w_decay_b, b_decay_b, gla_norm_g, gmlp_ln_g, gmlp_ln_b,
           w_spatial, b_spatial, w_out, norm2_g, w_gate, w_up, w_down, final_norm_g):
    depth = norm1_g.shape[0]
    for l in range(depth):
        x = _layer(x, norm1_g[l], w_in[l], w_decay_f[l], b_decay_f[l], w_decay_b[l], b_decay_b[l],
                   gla_norm_g[l], gmlp_ln_g[l], gmlp_ln_b[l], w_spatial[l], b_spatial[l], w_out[l],
                   norm2_g[l], w_gate[l], w_up[l], w_down[l], final_norm_g, final_norm=(l == depth - 1))
    return x
```

```python
import functools

import jax
import jax.numpy as jnp
from jax import lax
from jax.experimental import pallas as pl
from jax.experimental.pallas import tpu as pltpu

F32 = jnp.float32
BF16 = jnp.bfloat16

D_MODEL = 1024
GLA_HEADS = 4
GLA_PAIRS = GLA_HEADS // 2
GLA_DK = 64
GLA_DV = 128
GLA_KEY_WIDTH = GLA_HEADS * GLA_DK
GLA_WIDTH = GLA_HEADS * GLA_DV
GLA_LOWRANK = 16
GLA_TAU = 16.0
GLA_CHUNK = 64
GMLP_GROUPS = 4
GMLP_GROUP_DIM = 128
GMLP_WIDTH = GMLP_GROUPS * GMLP_GROUP_DIM
GMLP_CHUNK = 128
D_FF = 2816
EPS = 1e-6

LANES = 128
MXU_TILE = 256
DOT_COLS = 2 * MXU_TILE
COL_Q = 0
COL_K = COL_Q + GLA_KEY_WIDTH
COL_V = COL_K + GLA_KEY_WIDTH
COL_G = COL_V + GLA_WIDTH
COL_UV = COL_G + GLA_WIDTH
COL_LR = COL_UV + 2 * GMLP_WIDTH
PROJ_PAD = COL_LR + LANES
QKVG_WIDTH = COL_UV
UVLR_WIDTH = PROJ_PAD - COL_UV

SEQ_TILE = 512
HALF_TILE = SEQ_TILE // 2
N_CHUNKS = SEQ_TILE // GLA_CHUNK
N_GMLP = SEQ_TILE // GMLP_CHUNK
FF_BOUNDS = list(range(0, D_FF, DOT_COLS)) + [D_FF]
VMEM_LIMIT_BYTES = 60 * 1024 * 1024

_NT = (((1,), (1,)), ((), ()))
_TN = (((0,), (0,)), ((), ()))


def _dot(a, b):
    return jnp.dot(a, b, preferred_element_type=F32)


def _rms(x, g):
    return x * lax.rsqrt(jnp.mean(x * x, axis=-1, keepdims=True) + EPS) * g


def _log_sigmoid(z):
    return jnp.minimum(z, 0.0) - jnp.log(1.0 + jnp.exp(-jnp.abs(z)))


def _sigmoid(z):
    return 1.0 / (1.0 + jnp.exp(-z))


def _gelu(x):
    return 0.5 * x * (1.0 + lax.erf(x * (0.5 ** 0.5)))


def _interleave(*streams, windows=None):
    windows = windows or [(0.0, 1.0)] * len(streams)
    totals = [float(sum(c for c, _ in s)) or 1.0 for s in streams]
    pos = [0] * len(streams)
    done = [0.0] * len(streams)

    def when(k):
        lo, hi = windows[k]
        return lo + (hi - lo) * (done[k] + 0.5 * streams[k][pos[k]][0]) / totals[k]

    while True:
        live = [k for k in range(len(streams)) if pos[k] < len(streams[k])]
        if not live:
            return
        k = min(live, key=when)
        cost, thunk = streams[k][pos[k]]
        thunk()
        done[k] += cost
        pos[k] += 1


def _skewed(first, second, order):
    seq = []
    for k, item in enumerate(order):
        seq.append(first(item))
        if k >= 1:
            seq.append(second(order[k - 1]))
    seq.append(second(order[-1]))
    return seq


def _gla_stages(read, lr_read, wdec_ref, bdec_ref, state_ref, b_scr, dec_scr, o_write, reverse):
    last = 0 if reverse else GLA_CHUNK - 1
    keep = {}

    def decay_dot(r0):
        def f():
            keep["z", r0] = _dot(lr_read(slice(r0, r0 + HALF_TILE)), wdec_ref[...])
        return (150, f)

    def decay_cumsum(r0):
        def f():
            la = _log_sigmoid(keep.pop(("z", r0)) + bdec_ref[...]) * (1.0 / GLA_TAU)
            r = lax.broadcasted_iota(jnp.int32, (HALF_TILE, HALF_TILE), 0)
            c = lax.broadcasted_iota(jnp.int32, (HALF_TILE, HALF_TILE), 1)
            same_chunk = (r // GLA_CHUNK) == (c // GLA_CHUNK)
            tri = jnp.where(jnp.logical_and(same_chunk, (c >= r) if reverse else (c <= r)), 1.0, 0.0).astype(BF16)
            hi = la.astype(BF16)
            lo = (la - hi.astype(F32)).astype(BF16)
            both = _dot(tri, jnp.concatenate([hi, lo], axis=1))
            b = both[:, :GLA_KEY_WIDTH] + both[:, GLA_KEY_WIDTH:]
            for p in range(GLA_PAIRS):
                b_scr[p, r0:r0 + HALF_TILE, :] = b[:, p * LANES:(p + 1) * LANES]
        return (400, f)

    def chunk_decays():
        pad = jnp.zeros((LANES - N_CHUNKS, LANES), F32)
        for p in range(GLA_PAIRS):
            b_last = b_scr[p, pl.ds(last, N_CHUNKS, stride=GLA_CHUNK), :]
            dec_scr[p] = jnp.exp(jnp.concatenate([b_last, pad], axis=0).T)

    halves = [HALF_TILE, 0] if reverse else [0, HALF_TILE]
    decay_stages = [decay_dot(r0) for r0 in halves] + [decay_cumsum(r0) for r0 in halves] + [(100, chunk_decays)]

    def first(c):
        def f():
            ci = lax.broadcasted_iota(jnp.int32, (GLA_CHUNK, LANES), 1)
            first_head = ci < GLA_DK
            zeros_k = jnp.zeros((GLA_CHUNK, LANES), F32)
            rows = slice(c * GLA_CHUNK, (c + 1) * GLA_CHUNK)
            b = jnp.concatenate([b_scr[p, rows, :] for p in range(GLA_PAIRS)], axis=1)
            e_last = jnp.exp(b[last:last + 1, :])
            qd = (read(rows, slice(COL_Q, COL_K)) * (jnp.exp(b) * GLA_DK ** -0.5)).astype(BF16)
            kd = read(rows, slice(COL_K, COL_V)) * jnp.exp(-b)
            kte = (kd * e_last).astype(BF16)
            vb = read(rows, slice(COL_V, COL_G)).astype(BF16)
            blocks = []
            for p in range(GLA_PAIRS):
                kd_p = kd[:, p * LANES:(p + 1) * LANES]
                for blk in (jnp.where(first_head, kd_p, 0.0), jnp.where(first_head, 0.0, kd_p)):
                    blocks.append(jnp.concatenate([blk, zeros_k] if p == 0 else [zeros_k, blk], axis=1))
            kd_bd = jnp.concatenate(blocks, axis=0).astype(BF16)
            scores = lax.dot_general(qd, kd_bd, _NT, preferred_element_type=F32)
            d_state = [lax.dot_general(kte[:, p * LANES:(p + 1) * LANES],
                                       vb[:, p * 2 * GLA_DV:(p + 1) * 2 * GLA_DV], _TN,
                                       preferred_element_type=F32) for p in range(GLA_PAIRS)]
            keep[c] = (qd, vb, scores, d_state)
        return (260, f)

    def second(c):
        def f():
            ri = lax.broadcasted_iota(jnp.int32, (GLA_CHUNK, GLA_KEY_WIDTH), 0)
            ti = lax.broadcasted_iota(jnp.int32, (GLA_CHUNK, GLA_KEY_WIDTH), 1) & (GLA_DK - 1)
            causal = (ti >= ri) if reverse else (ti <= ri)
            zeros_v = jnp.zeros((GLA_CHUNK, GLA_DV), BF16)
            qd, vb, scores, d_state = keep.pop(c)
            a = jnp.where(causal, scores, 0.0).astype(BF16)
            for p in range(GLA_PAIRS):
                h0, h1 = 2 * p, 2 * p + 1
                lanes = slice(p * LANES, (p + 1) * LANES)
                s0 = state_ref[h0 * GLA_DK:(h0 + 1) * GLA_DK, :]
                s1 = state_ref[h1 * GLA_DK:(h1 + 1) * GLA_DK, :]
                rhs = jnp.concatenate([
                    jnp.concatenate([vb[:, h0 * GLA_DV:(h0 + 1) * GLA_DV], zeros_v], axis=1),
                    jnp.concatenate([zeros_v, vb[:, h1 * GLA_DV:(h1 + 1) * GLA_DV]], axis=1),
                    jnp.concatenate([s0.astype(BF16), zeros_v], axis=1),
                    jnp.concatenate([zeros_v, s1.astype(BF16)], axis=1)], axis=0)
                o_write(c, p, _dot(jnp.concatenate([a[:, lanes], qd[:, lanes]], axis=1), rhs))
                dec0 = dec_scr[p, :GLA_DK, c:c + 1]
                dec1 = dec_scr[p, GLA_DK:, c:c + 1]
                state_ref[h0 * GLA_DK:(h0 + 1) * GLA_DK, :] = (
                    s0 * dec0 + d_state[p][:GLA_DK, :GLA_DV])
                state_ref[h1 * GLA_DK:(h1 + 1) * GLA_DK, :] = (
                    s1 * dec1 + d_state[p][GLA_DK:, GLA_DV:])
        return (120, f)

    return decay_stages, first, second


def _fwd_sweep_kernel(x_ref, n1g_ref, wqkvg_ref, wuvlr_ref, wdec_ref, bdec_ref, lng_ref, lnb_ref, wsp_ref, bsp_ref,
                      wg32_ref, wu32_ref, wd32_ref, wo32_ref,
                      qkvg_ref, lr_ref, of_ref, yb_ref, wg16_ref, wu16_ref, wd16_ref, wo16_ref,
                      p_even, p_odd, h_scr, vn_scr, state_ref, b_scr, dec_scr, *, n_tiles, tiles_per_seq):
    i = pl.program_id(0)

    def cast_stages():
        def cast(src, dst):
            def f():
                dst[...] = src[...].astype(BF16)
            return (40, f)

        return [cast(wg32_ref, wg16_ref), cast(wu32_ref, wu16_ref), cast(wd32_ref, wd16_ref),
                cast(wo32_ref, wo16_ref)]

    def project_stages(p_w):
        def norm(r0):
            def f():
                h_scr[r0:r0 + HALF_TILE, :] = _rms(x_ref[r0:r0 + HALF_TILE, :], n1g_ref[...]).astype(BF16)
            return (300, f)

        def piece(r0, w_ref, base, lo, hi):
            def f():
                rows = slice(r0, r0 + HALF_TILE)
                val = _dot(h_scr[rows, :], w_ref[:, lo:hi])
                p_w[rows, base + lo:base + hi] = val
                if w_ref is wqkvg_ref:
                    qkvg_ref[rows, lo:hi] = val.astype(BF16)
                elif base + hi == PROJ_PAD:
                    lr_ref[rows, :] = val[:, COL_LR - base - lo:].astype(BF16)
            return (hi - lo, f)

        pieces = [(wqkvg_ref, 0, lo, lo + DOT_COLS) for lo in range(0, QKVG_WIDTH, DOT_COLS)]
        pieces += [(wuvlr_ref, COL_UV, 0, DOT_COLS), (wuvlr_ref, COL_UV, DOT_COLS, UVLR_WIDTH)]
        lower = [piece(0, *pc) for pc in pieces]
        upper = [piece(HALF_TILE, *pc) for pc in pieces]
        return [norm(0), lower[0], norm(HALF_TILE)] + lower[1:] + upper

    def mix_stages(p_r):
        def o_write(c, pair, val):
            of_ref[c * GLA_CHUNK:(c + 1) * GLA_CHUNK, pair * 2 * GLA_DV:(pair + 1) * 2 * GLA_DV] = val

        decay_stages, first, second = _gla_stages(
            lambda rows, cols: p_r[rows, cols], lambda rows: p_r[rows, COL_LR:PROJ_PAD].astype(BF16),
            wdec_ref, bdec_ref, state_ref, b_scr, dec_scr, o_write, reverse=False)

        keep = {}

        def gmlp_norm(n):
            def f():
                rows = slice(n * GMLP_CHUNK, (n + 1) * GMLP_CHUNK)
                vg = _gelu(p_r[rows, COL_UV + GMLP_WIDTH:COL_LR])
                mu = jnp.mean(vg, axis=-1, keepdims=True)
                vc = vg - mu
                vn = vc * lax.rsqrt(jnp.mean(vc * vc, axis=-1, keepdims=True) + EPS) * lng_ref[...] + lnb_ref[...]
                for g in range(GMLP_GROUPS):
                    vn_scr[g, :, n * GMLP_CHUNK:(n + 1) * GMLP_CHUNK] = (
                        vn[:, g * GMLP_GROUP_DIM:(g + 1) * GMLP_GROUP_DIM].astype(BF16))
            return (300, f)

        def gmlp_mix():
            for g in range(GMLP_GROUPS):
                keep[g] = _dot(wsp_ref[g], vn_scr[g]) + bsp_ref[g]

        def gmlp_gate(n):
            def f():
                rows = slice(n * GMLP_CHUNK, (n + 1) * GMLP_CHUNK)
                for g in range(GMLP_GROUPS):
                    cols = slice(g * GMLP_GROUP_DIM, (g + 1) * GMLP_GROUP_DIM)
                    u = _gelu(p_r[rows, COL_UV + g * GMLP_GROUP_DIM:COL_UV + (g + 1) * GMLP_GROUP_DIM])
                    yb_ref[rows, cols] = (u * keep[g][:, n * GMLP_CHUNK:(n + 1) * GMLP_CHUNK]).astype(BF16)
            return (200, f)

        chunks = _skewed(first, second, list(range(N_CHUNKS)))
        norms = [gmlp_norm(n) for n in range(N_GMLP)]
        gates = [gmlp_gate(n) for n in range(N_GMLP)]
        stages = list(decay_stages)
        per = (len(chunks) - N_GMLP) // N_GMLP
        for n in range(N_GMLP):
            stages += chunks[n * per:(n + 1) * per] + [norms[n]]
        stages.append((300, gmlp_mix))
        for n, st in enumerate(chunks[N_GMLP * per:]):
            stages += [st, gates[n]]
        return stages

    @pl.when(lax.rem(i - 1, tiles_per_seq) == 0)
    def _():
        state_ref[...] = jnp.zeros_like(state_ref)

    steady = jnp.logical_and(i > 0, i < n_tiles)
    windows = [(0.0, 1.0), (0.0, 0.94), (0.2, 0.8)]

    @pl.when(i == 0)
    def _():
        _interleave(project_stages(p_even), cast_stages())

    @pl.when(jnp.logical_and(steady, lax.rem(i, 2) == 0))
    def _():
        _interleave(project_stages(p_even), mix_stages(p_odd), cast_stages(), windows=windows)

    @pl.when(jnp.logical_and(steady, lax.rem(i, 2) == 1))
    def _():
        _interleave(project_stages(p_odd), mix_stages(p_even), cast_stages(), windows=windows)

    @pl.when(i == n_tiles)
    def _():
        _interleave(mix_stages(p_even if (n_tiles - 1) % 2 == 0 else p_odd), cast_stages())


def _bwd_sweep_kernel(qkvg_ref, lr_ref, of_ref, yb_ref, x_ref, wdec_ref, bdec_ref, gng_ref, wout_ref,
                      n2g_ref, wg_ref, wu_ref, wd_ref, fng_ref,
                      out_ref,
                      x2_even, x2_odd, h2_scr, a_scr, acc_scr, o_scr, y_scr, state_ref, b_scr, dec_scr,
                      *, n_tiles, tiles_per_seq, final_norm):
    j = pl.program_id(0)

    def mix_stages(x2_w):
        def o_write(c, pair, val):
            o_scr[c * GLA_CHUNK:(c + 1) * GLA_CHUNK, pair * 2 * GLA_DV:(pair + 1) * 2 * GLA_DV] = val

        decay_stages, first, second = _gla_stages(
            lambda rows, cols: qkvg_ref[rows, cols].astype(F32), lambda rows: lr_ref[rows, :],
            wdec_ref, bdec_ref, state_ref, b_scr, dec_scr, o_write, reverse=True)

        def gate(r0):
            def f():
                rows = slice(r0, r0 + HALF_TILE)
                g = qkvg_ref[rows, COL_G:COL_UV].astype(F32)
                g = g * _sigmoid(g)
                for hd in range(GLA_HEADS):
                    cols = slice(hd * GLA_DV, (hd + 1) * GLA_DV)
                    o = of_ref[rows, cols] + o_scr[rows, cols]
                    o = o * lax.rsqrt(jnp.mean(o * o, axis=-1, keepdims=True) + EPS)
                    y_scr[rows, cols] = (o * gng_ref[:, cols] * g[:, cols]).astype(BF16)
                y_scr[rows, GLA_WIDTH:] = yb_ref[rows, :]
            return (250, f)

        def out_proj(r0, lo):
            def f():
                rows = slice(r0, r0 + HALF_TILE)
                x2_w[rows, lo:lo + DOT_COLS] = (x_ref[rows, lo:lo + DOT_COLS]
                                                + _dot(y_scr[rows, :], wout_ref[:, lo:lo + DOT_COLS]))
            return (512, f)

        chunks = _skewed(first, second, list(range(N_CHUNKS - 1, -1, -1)))
        per_half = len(chunks) // 2
        upper = [gate(HALF_TILE)] + [out_proj(HALF_TILE, lo) for lo in range(0, D_MODEL, DOT_COLS)]
        lower = [gate(0)] + [out_proj(0, lo) for lo in range(0, D_MODEL, DOT_COLS)]
        tail = list(chunks[per_half + 1:])
        stages = decay_stages + chunks[:per_half + 1]
        for k, st in enumerate(upper):
            stages.append(st)
            stages += tail[2 * k:2 * k + 2]
        stages += tail[2 * len(upper):]
        return stages + lower

    def ffn_stages(x2_r):
        def norm(r0):
            def f():
                h2_scr[r0:r0 + HALF_TILE, :] = _rms(x2_r[r0:r0 + HALF_TILE, :], n2g_ref[...]).astype(BF16)
            return (300, f)

        def up(item):
            r0, c = item
            lo, hi = FF_BOUNDS[c], FF_BOUNDS[c + 1]

            def f():
                rows = slice(r0, r0 + HALF_TILE)
                h2 = h2_scr[rows, :]
                g = _dot(h2, wg_ref[:, lo:hi])
                u = _dot(h2, wu_ref[:, lo:hi])
                a_scr[c % 2, rows, :hi - lo] = (g * _sigmoid(g) * u).astype(BF16)
            return (2 * (hi - lo), f)

        def down(item):
            r0, c = item
            lo, hi = FF_BOUNDS[c], FF_BOUNDS[c + 1]

            def f():
                rows = slice(r0, r0 + HALF_TILE)
                d = _dot(a_scr[c % 2, rows, :hi - lo], wd_ref[lo:hi, :])
                if c == 0:
                    acc_scr[rows, :] = d
                else:
                    acc_scr[rows, :] += d
            return (hi - lo, f)

        def final(r0):
            def f():
                rows = slice(r0, r0 + HALF_TILE)
                y = x2_r[rows, :] + acc_scr[rows, :]
                if final_norm:
                    y = _rms(y, fng_ref[...])
                out_ref[rows, :] = y
            return (300, f)

        groups = range(len(FF_BOUNDS) - 1)
        lower = _skewed(up, down, [(0, c) for c in groups])
        upper = _skewed(up, down, [(HALF_TILE, c) for c in groups])
        return ([norm(0), lower[0], norm(HALF_TILE)] + lower[1:] + upper[:2] + [final(0)] + upper[2:]
                + [final(HALF_TILE)])

    @pl.when(lax.rem(j, tiles_per_seq) == 0)
    def _():
        state_ref[...] = jnp.zeros_like(state_ref)

    steady = jnp.logical_and(j > 0, j < n_tiles)

    @pl.when(j == 0)
    def _():
        _interleave(mix_stages(x2_even))

    @pl.when(jnp.logical_and(steady, lax.rem(j, 2) == 0))
    def _():
        _interleave(ffn_stages(x2_odd), mix_stages(x2_even))

    @pl.when(jnp.logical_and(steady, lax.rem(j, 2) == 1))
    def _():
        _interleave(ffn_stages(x2_even), mix_stages(x2_odd))

    @pl.when(j == n_tiles)
    def _():
        _interleave(ffn_stages(x2_even if (n_tiles - 1) % 2 == 0 else x2_odd))


def _const_spec(shape):
    return pl.BlockSpec(shape, lambda *_: (0,) * len(shape), pipeline_mode=pl.Buffered(1))


def _params():
    return pltpu.CompilerParams(dimension_semantics=("arbitrary",), vmem_limit_bytes=VMEM_LIMIT_BYTES)


def _layer(x, n1g, w_in, wdf, bdf, wdb, bdb, gng, lng, lnb, wsp, bsp, w_out, n2g, wg, wu, wd, fng, final_norm):
    B, S, D = x.shape
    ts = SEQ_TILE
    assert S % ts == 0 and D == D_MODEL
    nt = S // ts
    n = B * nt
    x2d = x.reshape(B * S, D)

    lr0 = COL_UV
    w_qkvg = w_in[:, :lr0].astype(BF16)
    w_uvlr = jnp.concatenate([w_in[:, lr0 + 2 * GLA_LOWRANK:], w_in[:, lr0:lr0 + 2 * GLA_LOWRANK],
                              jnp.zeros((D, LANES - 2 * GLA_LOWRANK), w_in.dtype)], axis=1).astype(BF16)
    zpad = jnp.zeros((LANES - 2 * GLA_LOWRANK, GLA_KEY_WIDTH), F32)
    zlr = jnp.zeros((GLA_LOWRANK, GLA_KEY_WIDTH), F32)
    wdf_p = jnp.concatenate([wdf, zlr, zpad], axis=0).astype(BF16)
    wdb_p = jnp.concatenate([zlr, wdb, zpad], axis=0).astype(BF16)

    cur = lambda w: pl.BlockSpec((ts, w), lambda i: (jnp.minimum(i, n - 1), 0))
    prev = lambda w: pl.BlockSpec((ts, w), lambda i: (jnp.maximum(i - 1, 0), 0))

    def rows_spec(w):
        rows, cols = w.shape
        blk = next(r for r in range(16, rows + 1, 16) if rows % r == 0 and r * (n + 1) >= rows)
        return pl.BlockSpec((blk, cols), lambda i: (jnp.minimum(i, rows // blk - 1), 0))

    ffn_weights = [wg, wu, wd, w_out]
    gla_scratch = [pltpu.VMEM((GLA_KEY_WIDTH, GLA_DV), F32), pltpu.VMEM((GLA_PAIRS, ts, LANES), F32),
                   pltpu.VMEM((GLA_PAIRS, LANES, LANES), F32)]
    qkvg, lr, o_f, y_b, wg_b, wu_b, wd_b, wo_b = pl.pallas_call(
        functools.partial(_fwd_sweep_kernel, n_tiles=n, tiles_per_seq=nt),
        grid=(n + 1,),
        in_specs=[cur(D), _const_spec((1, D)), _const_spec((D, QKVG_WIDTH)), _const_spec((D, UVLR_WIDTH)),
                  _const_spec((LANES, GLA_KEY_WIDTH)),
                  _const_spec((1, GLA_KEY_WIDTH)), _const_spec((1, GMLP_WIDTH)), _const_spec((1, GMLP_WIDTH)),
                  _const_spec((GMLP_GROUPS, GMLP_CHUNK, GMLP_CHUNK)), _const_spec((GMLP_GROUPS, GMLP_CHUNK, 1))]
                 + [rows_spec(w) for w in ffn_weights],
        out_specs=[cur(QKVG_WIDTH), cur(LANES), prev(GLA_WIDTH), prev(GMLP_WIDTH)]
                  + [rows_spec(w) for w in ffn_weights],
        out_shape=[jax.ShapeDtypeStruct((B * S, QKVG_WIDTH), BF16), jax.ShapeDtypeStruct((B * S, LANES), BF16),
                   jax.ShapeDtypeStruct((B * S, GLA_WIDTH), F32), jax.ShapeDtypeStruct((B * S, GMLP_WIDTH), BF16)]
                  + [jax.ShapeDtypeStruct(w.shape, BF16) for w in ffn_weights],
        scratch_shapes=[pltpu.VMEM((ts, PROJ_PAD), F32), pltpu.VMEM((ts, PROJ_PAD), F32),
                        pltpu.VMEM((ts, D), BF16), pltpu.VMEM((GMLP_GROUPS, GMLP_CHUNK, ts), BF16)] + gla_scratch,
        compiler_params=_params(),
        name="fwd_sweep",
    )(x2d, n1g.reshape(1, D), w_qkvg, w_uvlr, wdf_p, bdf.reshape(1, -1), lng.reshape(1, -1), lnb.reshape(1, -1),
      wsp.astype(BF16), bsp.reshape(GMLP_GROUPS, GMLP_CHUNK, 1), *ffn_weights)

    def rblock(j):
        j = jnp.clip(j, 0, n - 1)
        return (j // nt) * nt + (nt - 1 - j % nt)

    rcur = lambda w: pl.BlockSpec((ts, w), lambda j: (rblock(j), 0))
    rprev = lambda w: pl.BlockSpec((ts, w), lambda j: (rblock(j - 1), 0))
    out = pl.pallas_call(
        functools.partial(_bwd_sweep_kernel, n_tiles=n, tiles_per_seq=nt, final_norm=final_norm),
        grid=(n + 1,),
        in_specs=[rcur(QKVG_WIDTH), rcur(LANES), rcur(GLA_WIDTH), rcur(GMLP_WIDTH), rcur(D),
                  _const_spec((LANES, GLA_KEY_WIDTH)), _const_spec((1, GLA_KEY_WIDTH)),
                  _const_spec((1, GLA_WIDTH)), _const_spec((D, D)),
                  _const_spec((1, D)), _const_spec((D, D_FF)), _const_spec((D, D_FF)), _const_spec((D_FF, D)),
                  _const_spec((1, D))],
        out_specs=rprev(D),
        out_shape=jax.ShapeDtypeStruct((B * S, D), F32),
        scratch_shapes=[pltpu.VMEM((ts, D), F32), pltpu.VMEM((ts, D), F32), pltpu.VMEM((ts, D), BF16),
                        pltpu.VMEM((2, ts, DOT_COLS), BF16), pltpu.VMEM((ts, D), F32),
                        pltpu.VMEM((ts, GLA_WIDTH), F32), pltpu.VMEM((ts, D), BF16)] + gla_scratch,
        compiler_params=_params(),
        name="bwd_sweep",
    )(qkvg, lr, o_f, y_b, x2d, wdb_p, bdb.reshape(1, -1), gng.reshape(1, -1), wo_b,
      n2g.reshape(1, D), wg_b, wu_b, wd_b, fng.reshape(1, D))
    return out.reshape(B, S, D)


def kernel(x, norm1_g, w_in, w_decay_f, b_decay_f, w_decay_b, b_decay_b, gla_norm_g, gmlp_ln_g, gmlp_ln_b,
           w_spatial, b_spatial, w_out, norm2_g, w_gate, w_up, w_down, final_norm_g):
    depth = norm1_g.shape[0]
    for l in range(depth):
        x = _layer(x, norm1_g[l], w_in[l], w_decay_f[l], b_decay_f[l], w_decay_b[l], b_decay_b[l],
                   gla_norm_g[l], gmlp_ln_g[l], gmlp_ln_b[l], w_spatial[l], b_spatial[l], w_out[l],
                   norm2_g[l], w_gate[l], w_up[l], w_down[l], final_norm_g, final_norm=(l == depth - 1))
    return x
```

```python
import functools

import jax
import jax.numpy as jnp
from jax import lax
from jax.experimental import pallas as pl
from jax.experimental.pallas import tpu as pltpu

F32 = jnp.float32
BF16 = jnp.bfloat16

D_MODEL = 1024
GLA_HEADS = 4
GLA_PAIRS = GLA_HEADS // 2
GLA_DK = 64
GLA_DV = 128
GLA_KEY_WIDTH = GLA_HEADS * GLA_DK
GLA_WIDTH = GLA_HEADS * GLA_DV
GLA_LOWRANK = 16
GLA_TAU = 16.0
GLA_CHUNK = 64
GMLP_GROUPS = 4
GMLP_GROUP_DIM = 128
GMLP_WIDTH = GMLP_GROUPS * GMLP_GROUP_DIM
GMLP_CHUNK = 128
D_FF = 2816
EPS = 1e-6

LANES = 128
MXU_TILE = 256
DOT_COLS = 2 * MXU_TILE
COL_Q = 0
COL_K = COL_Q + GLA_KEY_WIDTH
COL_V = COL_K + GLA_KEY_WIDTH
COL_G = COL_V + GLA_WIDTH
COL_UV = COL_G + GLA_WIDTH
COL_LR = COL_UV + 2 * GMLP_WIDTH
PROJ_PAD = COL_LR + LANES
QKVG_WIDTH = COL_UV
UVLR_WIDTH = PROJ_PAD - COL_UV

SEQ_TILE = 512
HALF_TILE = SEQ_TILE // 2
N_CHUNKS = SEQ_TILE // GLA_CHUNK
N_GMLP = SEQ_TILE // GMLP_CHUNK
FF_BOUNDS = list(range(0, D_FF, DOT_COLS)) + [D_FF]
VMEM_LIMIT_BYTES = 60 * 1024 * 1024

_NT = (((1,), (1,)), ((), ()))
_TN = (((0,), (0,)), ((), ()))


def _dot(a, b):
    return jnp.dot(a, b, preferred_element_type=F32)


def _rms(x, g):
    return x * lax.rsqrt(jnp.mean(x * x, axis=-1, keepdims=True) + EPS) * g


def _log_sigmoid(z):
    return jnp.minimum(z, 0.0) - jnp.log(1.0 + jnp.exp(-jnp.abs(z)))


def _sigmoid(z):
    return 1.0 / (1.0 + jnp.exp(-z))


def _gelu(x):
    return 0.5 * x * (1.0 + lax.erf(x * (0.5 ** 0.5)))


def _interleave(*streams, windows=None):
    windows = windows or [(0.0, 1.0)] * len(streams)
    totals = [float(sum(c for c, _ in s)) or 1.0 for s in streams]
    pos = [0] * len(streams)
    done = [0.0] * len(streams)

    def when(k):
        lo, hi = windows[k]
        return lo + (hi - lo) * (done[k] + 0.5 * streams[k][pos[k]][0]) / totals[k]

    while True:
        live = [k for k in range(len(streams)) if pos[k] < len(streams[k])]
        if not live:
            return
        k = min(live, key=when)
        cost, thunk = streams[k][pos[k]]
        thunk()
        done[k] += cost
        pos[k] += 1


def _skewed(first, second, order):
    seq = []
    for k, item in enumerate(order):
        seq.append(first(item))
        if k >= 1:
            seq.append(second(order[k - 1]))
    seq.append(second(order[-1]))
    return seq


def _gla_stages(read, lr_read, wdec_ref, bdec_ref, state_ref, b_scr, dec_scr, o_write, reverse):
    last = 0 if reverse else GLA_CHUNK - 1
    keep = {}

    def decay_dot(r0):
        def f():
            keep["z", r0] = _dot(lr_read(slice(r0, r0 + HALF_TILE)), wdec_ref[...])
        return (150, f)

    def decay_cumsum(r0):
        def f():
            la = _log_sigmoid(keep.pop(("z", r0)) + bdec_ref[...]) * (1.0 / GLA_TAU)
            r = lax.broadcasted_iota(jnp.int32, (HALF_TILE, HALF_TILE), 0)
            c = lax.broadcasted_iota(jnp.int32, (HALF_TILE, HALF_TILE), 1)
            same_chunk = (r // GLA_CHUNK) == (c // GLA_CHUNK)
            tri = jnp.where(jnp.logical_and(same_chunk, (c >= r) if reverse else (c <= r)), 1.0, 0.0).astype(BF16)
            hi = la.astype(BF16)
            lo = (la - hi.astype(F32)).astype(BF16)
            both = _dot(tri, jnp.concatenate([hi, lo], axis=1))
            b = both[:, :GLA_KEY_WIDTH] + both[:, GLA_KEY_WIDTH:]
            for p in range(GLA_PAIRS):
                b_scr[p, r0:r0 + HALF_TILE, :] = b[:, p * LANES:(p + 1) * LANES]
        return (400, f)

    def chunk_decays():
        pad = jnp.zeros((LANES - N_CHUNKS, LANES), F32)
        for p in range(GLA_PAIRS):
            b_last = b_scr[p, pl.ds(last, N_CHUNKS, stride=GLA_CHUNK), :]
            dec_scr[p] = jnp.exp(jnp.concatenate([b_last, pad], axis=0).T)

    halves = [HALF_TILE, 0] if reverse else [0, HALF_TILE]
    decay_stages = [decay_dot(r0) for r0 in halves] + [decay_cumsum(r0) for r0 in halves] + [(100, chunk_decays)]

    def first(c):
        def f():
            ci = lax.broadcasted_iota(jnp.int32, (GLA_CHUNK, LANES), 1)
            first_head = ci < GLA_DK
            rows = slice(c * GLA_CHUNK, (c + 1) * GLA_CHUNK)
            b = jnp.concatenate([b_scr[p, rows, :] for p in range(GLA_PAIRS)], axis=1)
            e_last = jnp.exp(b[last:last + 1, :])
            qd = (read(rows, slice(COL_Q, COL_K)) * (jnp.exp(b) * GLA_DK ** -0.5)).astype(BF16)
            kd = read(rows, slice(COL_K, COL_V)) * jnp.exp(-b)
            kte = (kd * e_last).astype(BF16)
            vb = read(rows, slice(COL_V, COL_G)).astype(BF16)
            scores = []
            for p in range(GLA_PAIRS):
                kd_p = kd[:, p * LANES:(p + 1) * LANES]
                kd_bd = jnp.concatenate([jnp.where(first_head, kd_p, 0.0), jnp.where(first_head, 0.0, kd_p)], axis=0)
                scores.append(_dot(qd[:, p * LANES:(p + 1) * LANES], kd_bd.T.astype(BF16)))
            scores = jnp.concatenate(scores, axis=1)
            d_state = [lax.dot_general(kte[:, p * LANES:(p + 1) * LANES],
                                       vb[:, p * 2 * GLA_DV:(p + 1) * 2 * GLA_DV], _TN,
                                       preferred_element_type=F32) for p in range(GLA_PAIRS)]
            keep[c] = (qd, vb, scores, d_state)
        return (260, f)

    def second(c):
        def f():
            ri = lax.broadcasted_iota(jnp.int32, (GLA_CHUNK, GLA_KEY_WIDTH), 0)
            ti = lax.broadcasted_iota(jnp.int32, (GLA_CHUNK, GLA_KEY_WIDTH), 1) & (GLA_DK - 1)
            causal = (ti >= ri) if reverse else (ti <= ri)
            zeros_v = jnp.zeros((GLA_CHUNK, GLA_DV), BF16)
            qd, vb, scores, d_state = keep.pop(c)
            a = jnp.where(causal, scores, 0.0).astype(BF16)
            for p in range(GLA_PAIRS):
                h0, h1 = 2 * p, 2 * p + 1
                lanes = slice(p * LANES, (p + 1) * LANES)
                s0 = state_ref[h0 * GLA_DK:(h0 + 1) * GLA_DK, :]
                s1 = state_ref[h1 * GLA_DK:(h1 + 1) * GLA_DK, :]
                rhs = jnp.concatenate([
                    jnp.concatenate([vb[:, h0 * GLA_DV:(h0 + 1) * GLA_DV], zeros_v], axis=1),
                    jnp.concatenate([zeros_v, vb[:, h1 * GLA_DV:(h1 + 1) * GLA_DV]], axis=1),
                    jnp.concatenate([s0.astype(BF16), zeros_v], axis=1),
                    jnp.concatenate([zeros_v, s1.astype(BF16)], axis=1)], axis=0)
                o_write(c, p, _dot(jnp.concatenate([a[:, lanes], qd[:, lanes]], axis=1), rhs))
                dec0 = dec_scr[p, :GLA_DK, c:c + 1]
                dec1 = dec_scr[p, GLA_DK:, c:c + 1]
                state_ref[h0 * GLA_DK:(h0 + 1) * GLA_DK, :] = (
                    s0 * dec0 + d_state[p][:GLA_DK, :GLA_DV])
                state_ref[h1 * GLA_DK:(h1 + 1) * GLA_DK, :] = (
                    s1 * dec1 + d_state[p][GLA_DK:, GLA_DV:])
        return (120, f)

    return decay_stages, first, second


def _fwd_sweep_kernel(x_ref, n1g_ref, wqkvg_ref, wuvlr_ref, wdec_ref, bdec_ref, lng_ref, lnb_ref, wsp_ref, bsp_ref,
                      wg32_ref, wu32_ref, wd32_ref, wo32_ref,
                      qkvg_ref, lr_ref, of_ref, yb_ref, wg16_ref, wu16_ref, wd16_ref, wo16_ref,
                      p_even, p_odd, h_scr, vn_scr, state_ref, b_scr, dec_scr, *, n_tiles, tiles_per_seq):
    i = pl.program_id(0)

    def cast_stages():
        def cast(src, dst):
            def f():
                dst[...] = src[...].astype(BF16)
            return (40, f)

        return [cast(wg32_ref, wg16_ref), cast(wu32_ref, wu16_ref), cast(wd32_ref, wd16_ref),
                cast(wo32_ref, wo16_ref)]

    def project_stages(p_w):
        def norm(r0):
            def f():
                h_scr[r0:r0 + HALF_TILE, :] = _rms(x_ref[r0:r0 + HALF_TILE, :], n1g_ref[...]).astype(BF16)
            return (300, f)

        def piece(r0, w_ref, base, lo, hi):
            def f():
                rows = slice(r0, r0 + HALF_TILE)
                val = _dot(h_scr[rows, :], w_ref[:, lo:hi])
                p_w[rows, base + lo:base + hi] = val
                if w_ref is wqkvg_ref:
                    qkvg_ref[rows, lo:hi] = val.astype(BF16)
                elif base + hi == PROJ_PAD:
                    lr_ref[rows, :] = val[:, COL_LR - base - lo:].astype(BF16)
            return (hi - lo, f)

        pieces = [(wqkvg_ref, 0, lo, lo + DOT_COLS) for lo in range(0, QKVG_WIDTH, DOT_COLS)]
        pieces += [(wuvlr_ref, COL_UV, 0, DOT_COLS), (wuvlr_ref, COL_UV, DOT_COLS, UVLR_WIDTH)]
        lower = [piece(0, *pc) for pc in pieces]
        upper = [piece(HALF_TILE, *pc) for pc in pieces]
        return [norm(0), lower[0], norm(HALF_TILE)] + lower[1:] + upper

    def mix_stages(p_r):
        def o_write(c, pair, val):
            of_ref[c * GLA_CHUNK:(c + 1) * GLA_CHUNK, pair * 2 * GLA_DV:(pair + 1) * 2 * GLA_DV] = val

        decay_stages, first, second = _gla_stages(
            lambda rows, cols: p_r[rows, cols], lambda rows: p_r[rows, COL_LR:PROJ_PAD].astype(BF16),
            wdec_ref, bdec_ref, state_ref, b_scr, dec_scr, o_write, reverse=False)

        keep = {}

        def gmlp_norm(n):
            def f():
                rows = slice(n * GMLP_CHUNK, (n + 1) * GMLP_CHUNK)
                vg = _gelu(p_r[rows, COL_UV + GMLP_WIDTH:COL_LR])
                mu = jnp.mean(vg, axis=-1, keepdims=True)
                vc = vg - mu
                vn = vc * lax.rsqrt(jnp.mean(vc * vc, axis=-1, keepdims=True) + EPS) * lng_ref[...] + lnb_ref[...]
                for g in range(GMLP_GROUPS):
                    vn_scr[g, :, n * GMLP_CHUNK:(n + 1) * GMLP_CHUNK] = (
                        vn[:, g * GMLP_GROUP_DIM:(g + 1) * GMLP_GROUP_DIM].astype(BF16))
            return (300, f)

        def gmlp_mix():
            for g in range(GMLP_GROUPS):
                keep[g] = _dot(wsp_ref[g], vn_scr[g]) + bsp_ref[g]

        def gmlp_gate(n):
            def f():
                rows = slice(n * GMLP_CHUNK, (n + 1) * GMLP_CHUNK)
                for g in range(GMLP_GROUPS):
                    cols = slice(g * GMLP_GROUP_DIM, (g + 1) * GMLP_GROUP_DIM)
                    u = _gelu(p_r[rows, COL_UV + g * GMLP_GROUP_DIM:COL_UV + (g + 1) * GMLP_GROUP_DIM])
                    yb_ref[rows, cols] = (u * keep[g][:, n * GMLP_CHUNK:(n + 1) * GMLP_CHUNK]).astype(BF16)
            return (200, f)

        chunks = _skewed(first, second, list(range(N_CHUNKS)))
        norms = [gmlp_norm(n) for n in range(N_GMLP)]
        gates = [gmlp_gate(n) for n in range(N_GMLP)]
        stages = list(decay_stages)
        per = (len(chunks) - N_GMLP) // N_GMLP
        for n in range(N_GMLP):
            stages += chunks[n * per:(n + 1) * per] + [norms[n]]
        stages.append((300, gmlp_mix))
        for n, st in enumerate(chunks[N_GMLP * per:]):
            stages += [st, gates[n]]
        return stages

    @pl.when(lax.rem(i - 1, tiles_per_seq) == 0)
    def _():
        state_ref[...] = jnp.zeros_like(state_ref)

    steady = jnp.logical_and(i > 0, i < n_tiles)
    windows = [(0.0, 1.0), (0.0, 0.94), (0.2, 0.8)]

    @pl.when(i == 0)
    def _():
        _interleave(project_stages(p_even), cast_stages())

    @pl.when(jnp.logical_and(steady, lax.rem(i, 2) == 0))
    def _():
        _interleave(project_stages(p_even), mix_stages(p_odd), cast_stages(), windows=windows)

    @pl.when(jnp.logical_and(steady, lax.rem(i, 2) == 1))
    def _():
        _interleave(project_stages(p_odd), mix_stages(p_even), cast_stages(), windows=windows)

    @pl.when(i == n_tiles)
    def _():
        _interleave(mix_stages(p_even if (n_tiles - 1) % 2 == 0 else p_odd), cast_stages())


def _bwd_sweep_kernel(qkvg_ref, lr_ref, of_ref, yb_ref, x_ref, wdec_ref, bdec_ref, gng_ref, wout_ref,
                      n2g_ref, wg_ref, wu_ref, wd_ref, fng_ref,
                      out_ref,
                      x2_even, x2_odd, h2_scr, a_scr, acc_scr, o_scr, y_scr, state_ref, b_scr, dec_scr,
                      *, n_tiles, tiles_per_seq, final_norm):
    j = pl.program_id(0)

    def mix_stages(x2_w):
        def o_write(c, pair, val):
            o_scr[c * GLA_CHUNK:(c + 1) * GLA_CHUNK, pair * 2 * GLA_DV:(pair + 1) * 2 * GLA_DV] = val

        decay_stages, first, second = _gla_stages(
            lambda rows, cols: qkvg_ref[rows, cols].astype(F32), lambda rows: lr_ref[rows, :],
            wdec_ref, bdec_ref, state_ref, b_scr, dec_scr, o_write, reverse=True)

        def gate(r0):
            def f():
                rows = slice(r0, r0 + HALF_TILE)
                g = qkvg_ref[rows, COL_G:COL_UV].astype(F32)
                g = g * _sigmoid(g)
                for hd in range(GLA_HEADS):
                    cols = slice(hd * GLA_DV, (hd + 1) * GLA_DV)
                    o = of_ref[rows, cols] + o_scr[rows, cols]
                    o = o * lax.rsqrt(jnp.mean(o * o, axis=-1, keepdims=True) + EPS)
                    y_scr[rows, cols] = (o * gng_ref[:, cols] * g[:, cols]).astype(BF16)
                y_scr[rows, GLA_WIDTH:] = yb_ref[rows, :]
            return (250, f)

        def out_proj(r0, lo):
            def f():
                rows = slice(r0, r0 + HALF_TILE)
                x2_w[rows, lo:lo + DOT_COLS] = (x_ref[rows, lo:lo + DOT_COLS]
                                                + _dot(y_scr[rows, :], wout_ref[:, lo:lo + DOT_COLS]))
            return (512, f)

        chunks = _skewed(first, second, list(range(N_CHUNKS - 1, -1, -1)))
        per_half = len(chunks) // 2
        upper = [gate(HALF_TILE)] + [out_proj(HALF_TILE, lo) for lo in range(0, D_MODEL, DOT_COLS)]
        lower = [gate(0)] + [out_proj(0, lo) for lo in range(0, D_MODEL, DOT_COLS)]
        tail = list(chunks[per_half + 1:])
        stages = decay_stages + chunks[:per_half + 1]
        for k, st in enumerate(upper):
            stages.append(st)
            stages += tail[2 * k:2 * k + 2]
        stages += tail[2 * len(upper):]
        return stages + lower

    def ffn_stages(x2_r):
        def norm(r0):
            def f():
                h2_scr[r0:r0 + HALF_TILE, :] = _rms(x2_r[r0:r0 + HALF_TILE, :], n2g_ref[...]).astype(BF16)
            return (300, f)

        def up(item):
            r0, c = item
            lo, hi = FF_BOUNDS[c], FF_BOUNDS[c + 1]

            def f():
                rows = slice(r0, r0 + HALF_TILE)
                h2 = h2_scr[rows, :]
                g = _dot(h2, wg_ref[:, lo:hi])
                u = _dot(h2, wu_ref[:, lo:hi])
                a_scr[c % 2, rows, :hi - lo] = (g * _sigmoid(g) * u).astype(BF16)
            return (2 * (hi - lo), f)

        def down(item):
            r0, c = item
            lo, hi = FF_BOUNDS[c], FF_BOUNDS[c + 1]

            def f():
                rows = slice(r0, r0 + HALF_TILE)
                d = _dot(a_scr[c % 2, rows, :hi - lo], wd_ref[lo:hi, :])
                if c == 0:
                    acc_scr[rows, :] = d
                else:
                    acc_scr[rows, :] += d
            return (hi - lo, f)

        def final(r0):
            def f():
                rows = slice(r0, r0 + HALF_TILE)
                y = x2_r[rows, :] + acc_scr[rows, :]
                if final_norm:
                    y = _rms(y, fng_ref[...])
                out_ref[rows, :] = y
            return (300, f)

        groups = range(len(FF_BOUNDS) - 1)
        lower = _skewed(up, down, [(0, c) for c in groups])
        upper = _skewed(up, down, [(HALF_TILE, c) for c in groups])
        return ([norm(0), lower[0], norm(HALF_TILE)] + lower[1:] + upper[:2] + [final(0)] + upper[2:]
                + [final(HALF_TILE)])

    @pl.when(lax.rem(j, tiles_per_seq) == 0)
    def _():
        state_ref[...] = jnp.zeros_like(state_ref)

    steady = jnp.logical_and(j > 0, j < n_tiles)

    @pl.when(j == 0)
    def _():
        _interleave(mix_stages(x2_even))

    @pl.when(jnp.logical_and(steady, lax.rem(j, 2) == 0))
    def _():
        _interleave(ffn_stages(x2_odd), mix_stages(x2_even))

    @pl.when(jnp.logical_and(steady, lax.rem(j, 2) == 1))
    def _():
        _interleave(ffn_stages(x2_even), mix_stages(x2_odd))

    @pl.when(j == n_tiles)
    def _():
        _interleave(ffn_stages(x2_even if (n_tiles - 1) % 2 == 0 else x2_odd))


def _const_spec(shape):
    return pl.BlockSpec(shape, lambda *_: (0,) * len(shape), pipeline_mode=pl.Buffered(1))


def _params():
    return pltpu.CompilerParams(dimension_semantics=("arbitrary",), vmem_limit_bytes=VMEM_LIMIT_BYTES)


def _layer(x, n1g, w_in, wdf, bdf, wdb, bdb, gng, lng, lnb, wsp, bsp, w_out, n2g, wg, wu, wd, fng, final_norm):
    B, S, D = x.shape
    ts = SEQ_TILE
    assert S % ts == 0 and D == D_MODEL
    nt = S // ts
    n = B * nt
    x2d = x.reshape(B * S, D)

    lr0 = COL_UV
    w_qkvg = w_in[:, :lr0].astype(BF16)
    w_uvlr = jnp.concatenate([w_in[:, lr0 + 2 * GLA_LOWRANK:], w_in[:, lr0:lr0 + 2 * GLA_LOWRANK],
                              jnp.zeros((D, LANES - 2 * GLA_LOWRANK), w_in.dtype)], axis=1).astype(BF16)
    zpad = jnp.zeros((LANES - 2 * GLA_LOWRANK, GLA_KEY_WIDTH), F32)
    zlr = jnp.zeros((GLA_LOWRANK, GLA_KEY_WIDTH), F32)
    wdf_p = jnp.concatenate([wdf, zlr, zpad], axis=0).astype(BF16)
    wdb_p = jnp.concatenate([zlr, wdb, zpad], axis=0).astype(BF16)

    cur = lambda w: pl.BlockSpec((ts, w), lambda i: (jnp.minimum(i, n - 1), 0))
    prev = lambda w: pl.BlockSpec((ts, w), lambda i: (jnp.maximum(i - 1, 0), 0))

    def rows_spec(w):
        rows, cols = w.shape
        blk = next(r for r in range(16, rows + 1, 16) if rows % r == 0 and r * (n + 1) >= rows)
        return pl.BlockSpec((blk, cols), lambda i: (jnp.minimum(i, rows // blk - 1), 0))

    ffn_weights = [wg, wu, wd, w_out]
    gla_scratch = [pltpu.VMEM((GLA_KEY_WIDTH, GLA_DV), F32), pltpu.VMEM((GLA_PAIRS, ts, LANES), F32),
                   pltpu.VMEM((GLA_PAIRS, LANES, LANES), F32)]
    qkvg, lr, o_f, y_b, wg_b, wu_b, wd_b, wo_b = pl.pallas_call(
        functools.partial(_fwd_sweep_kernel, n_tiles=n, tiles_per_seq=nt),
        grid=(n + 1,),
        in_specs=[cur(D), _const_spec((1, D)), _const_spec((D, QKVG_WIDTH)), _const_spec((D, UVLR_WIDTH)),
                  _const_spec((LANES, GLA_KEY_WIDTH)),
                  _const_spec((1, GLA_KEY_WIDTH)), _const_spec((1, GMLP_WIDTH)), _const_spec((1, GMLP_WIDTH)),
                  _const_spec((GMLP_GROUPS, GMLP_CHUNK, GMLP_CHUNK)), _const_spec((GMLP_GROUPS, GMLP_CHUNK, 1))]
                 + [rows_spec(w) for w in ffn_weights],
        out_specs=[cur(QKVG_WIDTH), cur(LANES), prev(GLA_WIDTH), prev(GMLP_WIDTH)]
                  + [rows_spec(w) for w in ffn_weights],
        out_shape=[jax.ShapeDtypeStruct((B * S, QKVG_WIDTH), BF16), jax.ShapeDtypeStruct((B * S, LANES), BF16),
                   jax.ShapeDtypeStruct((B * S, GLA_WIDTH), F32), jax.ShapeDtypeStruct((B * S, GMLP_WIDTH), BF16)]
                  + [jax.ShapeDtypeStruct(w.shape, BF16) for w in ffn_weights],
        scratch_shapes=[pltpu.VMEM((ts, PROJ_PAD), F32), pltpu.VMEM((ts, PROJ_PAD), F32),
                        pltpu.VMEM((ts, D), BF16), pltpu.VMEM((GMLP_GROUPS, GMLP_CHUNK, ts), BF16)] + gla_scratch,
        compiler_params=_params(),
        name="fwd_sweep",
    )(x2d, n1g.reshape(1, D), w_qkvg, w_uvlr, wdf_p, bdf.reshape(1, -1), lng.reshape(1, -1), lnb.reshape(1, -1),
      wsp.astype(BF16), bsp.reshape(GMLP_GROUPS, GMLP_CHUNK, 1), *ffn_weights)

    def rblock(j):
        j = jnp.clip(j, 0, n - 1)
        return (j // nt) * nt + (nt - 1 - j % nt)

    rcur = lambda w: pl.BlockSpec((ts, w), lambda j: (rblock(j), 0))
    rprev = lambda w: pl.BlockSpec((ts, w), lambda j: (rblock(j - 1), 0))
    out = pl.pallas_call(
        functools.partial(_bwd_sweep_kernel, n_tiles=n, tiles_per_seq=nt, final_norm=final_norm),
        grid=(n + 1,),
        in_specs=[rcur(QKVG_WIDTH), rcur(LANES), rcur(GLA_WIDTH), rcur(GMLP_WIDTH), rcur(D),
                  _const_spec((LANES, GLA_KEY_WIDTH)), _const_spec((1, GLA_KEY_WIDTH)),
                  _const_spec((1, GLA_WIDTH)), _const_spec((D, D)),
                  _const_spec((1, D)), _const_spec((D, D_FF)), _const_spec((D, D_FF)), _const_spec((D_FF, D)),
                  _const_spec((1, D))],
        out_specs=rprev(D),
        out_shape=jax.ShapeDtypeStruct((B * S, D), F32),
        scratch_shapes=[pltpu.VMEM((ts, D), F32), pltpu.VMEM((ts, D), F32), pltpu.VMEM((ts, D), BF16),
                        pltpu.VMEM((2, ts, DOT_COLS), BF16), pltpu.VMEM((ts, D), F32),
                        pltpu.VMEM((ts, GLA_WIDTH), F32), pltpu.VMEM((ts, D), BF16)] + gla_scratch,
        compiler_params=_params(),
        name="bwd_sweep",
    )(qkvg, lr, o_f, y_b, x2d, wdb_p, bdb.reshape(1, -1), gng.reshape(1, -1), wo_b,
      n2g.reshape(1, D), wg_b, wu_b, wd_b, fng.reshape(1, D))
    return out.reshape(B, S, D)


def kernel(x, norm1_g, w_in, w_decay_f, b_decay_f, w_decay_b, b_decay_b, gla_norm_g, gmlp_ln_g, gmlp_ln_b,
           w_spatial, b_spatial, w_out, norm2_g, w_gate, w_up, w_down, final_norm_g):
    depth = norm1_g.shape[0]
    for l in range(depth):
        x = _layer(x, norm1_g[l], w_in[l], w_decay_f[l], b_decay_f[l], w_decay_b[l], b_decay_b[l],
                   gla_norm_g[l], gmlp_ln_g[l], gmlp_ln_b[l], w_spatial[l], b_spatial[l], w_out[l],
                   norm2_g[l], w_gate[l], w_up[l], w_down[l], final_norm_g, final_norm=(l == depth - 1))
    return x
```

```python
import functools

import jax
import jax.numpy as jnp
from jax import lax
from jax.experimental import pallas as pl
from jax.experimental.pallas import tpu as pltpu

F32 = jnp.float32
BF16 = jnp.bfloat16

D_MODEL = 1024
GLA_HEADS = 4
GLA_PAIRS = GLA_HEADS // 2
GLA_DK = 64
GLA_DV = 128
GLA_KEY_WIDTH = GLA_HEADS * GLA_DK
GLA_WIDTH = GLA_HEADS * GLA_DV
GLA_LOWRANK = 16
GLA_TAU = 16.0
GLA_CHUNK = 64
GMLP_GROUPS = 4
GMLP_GROUP_DIM = 128
GMLP_WIDTH = GMLP_GROUPS * GMLP_GROUP_DIM
GMLP_CHUNK = 128
D_FF = 2816
EPS = 1e-6

LANES = 128
MXU_TILE = 256
DOT_COLS = 2 * MXU_TILE
COL_Q = 0
COL_K = COL_Q + GLA_KEY_WIDTH
COL_V = COL_K + GLA_KEY_WIDTH
COL_G = COL_V + GLA_WIDTH
COL_UV = COL_G + GLA_WIDTH
COL_LR = COL_UV + 2 * GMLP_WIDTH
PROJ_PAD = COL_LR + LANES
QKVG_WIDTH = COL_UV
UVLR_WIDTH = PROJ_PAD - COL_UV

SEQ_TILE = 512
HALF_TILE = SEQ_TILE // 2
N_CHUNKS = SEQ_TILE // GLA_CHUNK
N_GMLP = SEQ_TILE // GMLP_CHUNK
FF_BOUNDS = list(range(0, D_FF, DOT_COLS)) + [D_FF]
VMEM_LIMIT_BYTES = 60 * 1024 * 1024

_TN = (((0,), (0,)), ((), ()))


def _dot(a, b):
    return jnp.dot(a, b, preferred_element_type=F32)


def _rms(x, g):
    return x * lax.rsqrt(jnp.mean(x * x, axis=-1, keepdims=True) + EPS) * g


def _log_sigmoid(z):
    return jnp.minimum(z, 0.0) - jnp.log(1.0 + jnp.exp(-jnp.abs(z)))


def _sigmoid(z):
    return 0.5 * (1.0 + jnp.tanh(0.5 * z))


def _gelu(x):
    return 0.5 * x * (1.0 + lax.erf(x * (0.5 ** 0.5)))


def _interleave(*streams, windows=None):
    windows = windows or [(0.0, 1.0)] * len(streams)
    totals = [float(sum(c for c, _ in s)) or 1.0 for s in streams]
    pos = [0] * len(streams)
    done = [0.0] * len(streams)

    def when(k):
        lo, hi = windows[k]
        return lo + (hi - lo) * (done[k] + 0.5 * streams[k][pos[k]][0]) / totals[k]

    while True:
        live = [k for k in range(len(streams)) if pos[k] < len(streams[k])]
        if not live:
            return
        k = min(live, key=when)
        cost, thunk = streams[k][pos[k]]
        thunk()
        done[k] += cost
        pos[k] += 1


def _skewed(first, second, order):
    seq = []
    for k, item in enumerate(order):
        seq.append(first(item))
        if k >= 1:
            seq.append(second(order[k - 1]))
    seq.append(second(order[-1]))
    return seq


def _gla_stages(read, lr_read, wdec_ref, bdec_ref, state_ref, b_scr, dec_scr, o_write, reverse):
    last = 0 if reverse else GLA_CHUNK - 1
    keep = {}

    def decay_dot(r0):
        def f():
            keep["z", r0] = _dot(lr_read(slice(r0, r0 + HALF_TILE)), wdec_ref[...])
        return (150, f)

    def decay_cumsum(r0):
        def f():
            la = _log_sigmoid(keep.pop(("z", r0)) + bdec_ref[...]) * (1.0 / GLA_TAU)
            r = lax.broadcasted_iota(jnp.int32, (HALF_TILE, HALF_TILE), 0)
            c = lax.broadcasted_iota(jnp.int32, (HALF_TILE, HALF_TILE), 1)
            same_chunk = (r // GLA_CHUNK) == (c // GLA_CHUNK)
            tri = jnp.where(jnp.logical_and(same_chunk, (c >= r) if reverse else (c <= r)), 1.0, 0.0).astype(BF16)
            hi = la.astype(BF16)
            lo = (la - hi.astype(F32)).astype(BF16)
            both = _dot(tri, jnp.concatenate([hi, lo], axis=1))
            b = both[:, :GLA_KEY_WIDTH] + both[:, GLA_KEY_WIDTH:]
            for p in range(GLA_PAIRS):
                b_scr[p, r0:r0 + HALF_TILE, :] = b[:, p * LANES:(p + 1) * LANES]
        return (400, f)

    def chunk_decays():
        pad = jnp.zeros((LANES - N_CHUNKS, LANES), F32)
        for p in range(GLA_PAIRS):
            b_last = b_scr[p, pl.ds(last, N_CHUNKS, stride=GLA_CHUNK), :]
            dec_scr[p] = jnp.exp(jnp.concatenate([b_last, pad], axis=0).T)

    halves = [HALF_TILE, 0] if reverse else [0, HALF_TILE]
    decay_stages = [decay_dot(r0) for r0 in halves] + [decay_cumsum(r0) for r0 in halves] + [(100, chunk_decays)]

    def first(c):
        def f():
            ci = lax.broadcasted_iota(jnp.int32, (GLA_CHUNK, LANES), 1)
            first_head = ci < GLA_DK
            rows = slice(c * GLA_CHUNK, (c + 1) * GLA_CHUNK)
            b = jnp.concatenate([b_scr[p, rows, :] for p in range(GLA_PAIRS)], axis=1)
            e_last = jnp.exp(b[last:last + 1, :])
            qd = (read(rows, slice(COL_Q, COL_K)) * (jnp.exp(b) * GLA_DK ** -0.5)).astype(BF16)
            kd = read(rows, slice(COL_K, COL_V)) * jnp.exp(-b)
            kte = (kd * e_last).astype(BF16)
            vb = read(rows, slice(COL_V, COL_G)).astype(BF16)
            scores = []
            for p in range(GLA_PAIRS):
                kd_p = kd[:, p * LANES:(p + 1) * LANES]
                kd_bd = jnp.concatenate([jnp.where(first_head, kd_p, 0.0), jnp.where(first_head, 0.0, kd_p)], axis=0)
                scores.append(_dot(qd[:, p * LANES:(p + 1) * LANES], kd_bd.T.astype(BF16)))
            scores = jnp.concatenate(scores, axis=1)
            d_state = [lax.dot_general(kte[:, p * LANES:(p + 1) * LANES],
                                       vb[:, p * 2 * GLA_DV:(p + 1) * 2 * GLA_DV], _TN,
                                       preferred_element_type=F32) for p in range(GLA_PAIRS)]
            keep[c] = (qd, vb, scores, d_state)
        return (220, f)

    def second(c):
        def f():
            ri = lax.broadcasted_iota(jnp.int32, (GLA_CHUNK, GLA_KEY_WIDTH), 0)
            ti = lax.broadcasted_iota(jnp.int32, (GLA_CHUNK, GLA_KEY_WIDTH), 1) & (GLA_DK - 1)
            causal = (ti >= ri) if reverse else (ti <= ri)
            zeros_v = jnp.zeros((GLA_CHUNK, GLA_DV), BF16)
            qd, vb, scores, d_state = keep.pop(c)
            a = jnp.where(causal, scores, 0.0).astype(BF16)
            for p in range(GLA_PAIRS):
                h0, h1 = 2 * p, 2 * p + 1
                lanes = slice(p * LANES, (p + 1) * LANES)
                s0 = state_ref[h0 * GLA_DK:(h0 + 1) * GLA_DK, :]
                s1 = state_ref[h1 * GLA_DK:(h1 + 1) * GLA_DK, :]
                rhs = jnp.concatenate([
                    jnp.concatenate([vb[:, h0 * GLA_DV:(h0 + 1) * GLA_DV], zeros_v], axis=1),
                    jnp.concatenate([zeros_v, vb[:, h1 * GLA_DV:(h1 + 1) * GLA_DV]], axis=1),
                    jnp.concatenate([s0.astype(BF16), zeros_v], axis=1),
                    jnp.concatenate([zeros_v, s1.astype(BF16)], axis=1)], axis=0)
                o_write(c, p, _dot(jnp.concatenate([a[:, lanes], qd[:, lanes]], axis=1), rhs))
                dec0 = dec_scr[p, :GLA_DK, c:c + 1]
                dec1 = dec_scr[p, GLA_DK:, c:c + 1]
                state_ref[h0 * GLA_DK:(h0 + 1) * GLA_DK, :] = (
                    s0 * dec0 + d_state[p][:GLA_DK, :GLA_DV])
                state_ref[h1 * GLA_DK:(h1 + 1) * GLA_DK, :] = (
                    s1 * dec1 + d_state[p][GLA_DK:, GLA_DV:])
        return (160, f)

    return decay_stages, first, second


def _fwd_sweep_kernel(x_ref, n1g_ref, wqkvg_ref, wuvlr_ref, wdec_ref, bdec_ref, lng_ref, lnb_ref, wsp_ref, bsp_ref,
                      wg32_ref, wu32_ref, wd32_ref, wo32_ref,
                      qkvg_ref, lr_ref, of_ref, yb_ref, wg16_ref, wu16_ref, wd16_ref, wo16_ref,
                      p_even, p_odd, h_scr, vn_scr, state_ref, b_scr, dec_scr, *, n_tiles, tiles_per_seq):
    i = pl.program_id(0)

    def cast_stages():
        def cast(src, dst):
            def f():
                dst[...] = src[...].astype(BF16)
            return (40, f)

        return [cast(wg32_ref, wg16_ref), cast(wu32_ref, wu16_ref), cast(wd32_ref, wd16_ref),
                cast(wo32_ref, wo16_ref)]

    def project_stages(p_w):
        def norm(r0):
            def f():
                h_scr[r0:r0 + HALF_TILE, :] = _rms(x_ref[r0:r0 + HALF_TILE, :], n1g_ref[...]).astype(BF16)
            return (300, f)

        def piece(r0, w_ref, base, lo, hi):
            def f():
                rows = slice(r0, r0 + HALF_TILE)
                val = _dot(h_scr[rows, :], w_ref[:, lo:hi])
                p_w[rows, base + lo:base + hi] = val
                if w_ref is wqkvg_ref:
                    qkvg_ref[rows, lo:hi] = val.astype(BF16)
                elif base + hi == PROJ_PAD:
                    lr_ref[rows, :] = val[:, COL_LR - base - lo:].astype(BF16)
            return (hi - lo, f)

        pieces = [(wqkvg_ref, 0, lo, lo + DOT_COLS) for lo in range(0, QKVG_WIDTH, DOT_COLS)]
        pieces += [(wuvlr_ref, COL_UV, 0, DOT_COLS), (wuvlr_ref, COL_UV, DOT_COLS, UVLR_WIDTH)]
        lower = [piece(0, *pc) for pc in pieces]
        upper = [piece(HALF_TILE, *pc) for pc in pieces]
        return [norm(0), lower[0], norm(HALF_TILE)] + lower[1:] + upper

    def mix_stages(p_r):
        def o_write(c, pair, val):
            of_ref[c * GLA_CHUNK:(c + 1) * GLA_CHUNK, pair * 2 * GLA_DV:(pair + 1) * 2 * GLA_DV] = val

        decay_stages, first, second = _gla_stages(
            lambda rows, cols: p_r[rows, cols], lambda rows: p_r[rows, COL_LR:PROJ_PAD].astype(BF16),
            wdec_ref, bdec_ref, state_ref, b_scr, dec_scr, o_write, reverse=False)

        keep = {}

        def gmlp_norm(n):
            def f():
                rows = slice(n * GMLP_CHUNK, (n + 1) * GMLP_CHUNK)
                vg = _gelu(p_r[rows, COL_UV + GMLP_WIDTH:COL_LR])
                mu = jnp.mean(vg, axis=-1, keepdims=True)
                vc = vg - mu
                vn = vc * lax.rsqrt(jnp.mean(vc * vc, axis=-1, keepdims=True) + EPS) * lng_ref[...] + lnb_ref[...]
                for g in range(GMLP_GROUPS):
                    vn_scr[g, :, n * GMLP_CHUNK:(n + 1) * GMLP_CHUNK] = (
                        vn[:, g * GMLP_GROUP_DIM:(g + 1) * GMLP_GROUP_DIM].astype(BF16))
            return (300, f)

        def gmlp_mix():
            for g in range(GMLP_GROUPS):
                keep[g] = _dot(wsp_ref[g], vn_scr[g]) + bsp_ref[g]

        def gmlp_gate(n):
            def f():
                rows = slice(n * GMLP_CHUNK, (n + 1) * GMLP_CHUNK)
                for g in range(GMLP_GROUPS):
                    cols = slice(g * GMLP_GROUP_DIM, (g + 1) * GMLP_GROUP_DIM)
                    u = _gelu(p_r[rows, COL_UV + g * GMLP_GROUP_DIM:COL_UV + (g + 1) * GMLP_GROUP_DIM])
                    yb_ref[rows, cols] = (u * keep[g][:, n * GMLP_CHUNK:(n + 1) * GMLP_CHUNK]).astype(BF16)
            return (200, f)

        chunks = _skewed(first, second, list(range(N_CHUNKS)))
        norms = [gmlp_norm(n) for n in range(N_GMLP)]
        gates = [gmlp_gate(n) for n in range(N_GMLP)]
        stages = list(decay_stages)
        per = (len(chunks) - N_GMLP) // N_GMLP
        for n in range(N_GMLP):
            stages += chunks[n * per:(n + 1) * per] + [norms[n]]
        stages.append((300, gmlp_mix))
        for n, st in enumerate(chunks[N_GMLP * per:]):
            stages += [st, gates[n]]
        return stages

    @pl.when(lax.rem(i - 1, tiles_per_seq) == 0)
    def _():
        state_ref[...] = jnp.zeros_like(state_ref)

    steady = jnp.logical_and(i > 0, i < n_tiles)
    windows = [(0.0, 1.0), (0.0, 0.94), (0.2, 0.8)]

    @pl.when(i == 0)
    def _():
        _interleave(project_stages(p_even), cast_stages())

    @pl.when(jnp.logical_and(steady, lax.rem(i, 2) == 0))
    def _():
        _interleave(project_stages(p_even), mix_stages(p_odd), cast_stages(), windows=windows)

    @pl.when(jnp.logical_and(steady, lax.rem(i, 2) == 1))
    def _():
        _interleave(project_stages(p_odd), mix_stages(p_even), cast_stages(), windows=windows)

    @pl.when(i == n_tiles)
    def _():
        _interleave(mix_stages(p_even if (n_tiles - 1) % 2 == 0 else p_odd), cast_stages())


def _bwd_sweep_kernel(qkvg_ref, lr_ref, of_ref, yb_ref, x_ref, wdec_ref, bdec_ref, gng_ref, wout_ref,
                      n2g_ref, wg_ref, wu_ref, wd_ref, fng_ref,
                      out_ref,
                      x2_even, x2_odd, h2_scr, a_scr, acc_scr, o_scr, y_scr, state_ref, b_scr, dec_scr,
                      *, n_tiles, tiles_per_seq, final_norm):
    j = pl.program_id(0)

    def mix_stages(x2_w):
        def o_write(c, pair, val):
            o_scr[c * GLA_CHUNK:(c + 1) * GLA_CHUNK, pair * 2 * GLA_DV:(pair + 1) * 2 * GLA_DV] = val

        decay_stages, first, second = _gla_stages(
            lambda rows, cols: qkvg_ref[rows, cols].astype(F32), lambda rows: lr_ref[rows, :],
            wdec_ref, bdec_ref, state_ref, b_scr, dec_scr, o_write, reverse=True)

        def gate(r0):
            def f():
                rows = slice(r0, r0 + HALF_TILE)
                g = qkvg_ref[rows, COL_G:COL_UV].astype(F32)
                g = g * _sigmoid(g)
                for hd in range(GLA_HEADS):
                    cols = slice(hd * GLA_DV, (hd + 1) * GLA_DV)
                    o = of_ref[rows, cols] + o_scr[rows, cols]
                    o = o * lax.rsqrt(jnp.mean(o * o, axis=-1, keepdims=True) + EPS)
                    y_scr[rows, cols] = (o * gng_ref[:, cols] * g[:, cols]).astype(BF16)
                y_scr[rows, GLA_WIDTH:] = yb_ref[rows, :]
            return (250, f)

        def out_proj(r0, lo):
            def f():
                rows = slice(r0, r0 + HALF_TILE)
                x2_w[rows, lo:lo + DOT_COLS] = (x_ref[rows, lo:lo + DOT_COLS]
                                                + _dot(y_scr[rows, :], wout_ref[:, lo:lo + DOT_COLS]))
            return (512, f)

        chunks = _skewed(first, second, list(range(N_CHUNKS - 1, -1, -1)))
        per_half = len(chunks) // 2
        upper = [gate(HALF_TILE)] + [out_proj(HALF_TILE, lo) for lo in range(0, D_MODEL, DOT_COLS)]
        lower = [gate(0)] + [out_proj(0, lo) for lo in range(0, D_MODEL, DOT_COLS)]
        tail = list(chunks[per_half + 1:])
        stages = decay_stages + chunks[:per_half + 1]
        for k, st in enumerate(upper):
            stages.append(st)
            stages += tail[2 * k:2 * k + 2]
        stages += tail[2 * len(upper):]
        return stages + lower

    def ffn_stages(x2_r):
        def norm(r0):
            def f():
                h2_scr[r0:r0 + HALF_TILE, :] = _rms(x2_r[r0:r0 + HALF_TILE, :], n2g_ref[...]).astype(BF16)
            return (300, f)

        def up(item):
            r0, c = item
            lo, hi = FF_BOUNDS[c], FF_BOUNDS[c + 1]

            def f():
                rows = slice(r0, r0 + HALF_TILE)
                h2 = h2_scr[rows, :]
                g = _dot(h2, wg_ref[:, lo:hi])
                u = _dot(h2, wu_ref[:, lo:hi])
                a_scr[c % 2, rows, :hi - lo] = (g * _sigmoid(g) * u).astype(BF16)
            return (2 * (hi - lo), f)

        def down(item):
            r0, c = item
            lo, hi = FF_BOUNDS[c], FF_BOUNDS[c + 1]

            def f():
                rows = slice(r0, r0 + HALF_TILE)
                d = _dot(a_scr[c % 2, rows, :hi - lo], wd_ref[lo:hi, :])
                if c == 0:
                    acc_scr[rows, :] = d
                else:
                    acc_scr[rows, :] += d
            return (hi - lo, f)

        def final(r0):
            def f():
                rows = slice(r0, r0 + HALF_TILE)
                y = x2_r[rows, :] + acc_scr[rows, :]
                if final_norm:
                    y = _rms(y, fng_ref[...])
                out_ref[rows, :] = y
            return (300, f)

        groups = range(len(FF_BOUNDS) - 1)
        lower = _skewed(up, down, [(0, c) for c in groups])
        upper = _skewed(up, down, [(HALF_TILE, c) for c in groups])
        return ([norm(0), lower[0], norm(HALF_TILE)] + lower[1:] + upper[:2] + [final(0)] + upper[2:]
                + [final(HALF_TILE)])

    @pl.when(lax.rem(j, tiles_per_seq) == 0)
    def _():
        state_ref[...] = jnp.zeros_like(state_ref)

    steady = jnp.logical_and(j > 0, j < n_tiles)

    @pl.when(j == 0)
    def _():
        _interleave(mix_stages(x2_even))

    @pl.when(jnp.logical_and(steady, lax.rem(j, 2) == 0))
    def _():
        _interleave(ffn_stages(x2_odd), mix_stages(x2_even))

    @pl.when(jnp.logical_and(steady, lax.rem(j, 2) == 1))
    def _():
        _interleave(ffn_stages(x2_even), mix_stages(x2_odd))

    @pl.when(j == n_tiles)
    def _():
        _interleave(ffn_stages(x2_even if (n_tiles - 1) % 2 == 0 else x2_odd))


def _const_spec(shape):
    return pl.BlockSpec(shape, lambda *_: (0,) * len(shape), pipeline_mode=pl.Buffered(1))


def _params():
    return pltpu.CompilerParams(dimension_semantics=("arbitrary",), vmem_limit_bytes=VMEM_LIMIT_BYTES)


def _layer(x, n1g, w_in, wdf, bdf, wdb, bdb, gng, lng, lnb, wsp, bsp, w_out, n2g, wg, wu, wd, fng, final_norm):
    B, S, D = x.shape
    ts = SEQ_TILE
    assert S % ts == 0 and D == D_MODEL
    nt = S // ts
    n = B * nt
    x2d = x.reshape(B * S, D)

    lr0 = COL_UV
    w_qkvg = w_in[:, :lr0].astype(BF16)
    w_uvlr = jnp.concatenate([w_in[:, lr0 + 2 * GLA_LOWRANK:], w_in[:, lr0:lr0 + 2 * GLA_LOWRANK],
                              jnp.zeros((D, LANES - 2 * GLA_LOWRANK), w_in.dtype)], axis=1).astype(BF16)
    zpad = jnp.zeros((LANES - 2 * GLA_LOWRANK, GLA_KEY_WIDTH), F32)
    zlr = jnp.zeros((GLA_LOWRANK, GLA_KEY_WIDTH), F32)
    wdf_p = jnp.concatenate([wdf, zlr, zpad], axis=0).astype(BF16)
    wdb_p = jnp.concatenate([zlr, wdb, zpad], axis=0).astype(BF16)

    cur = lambda w: pl.BlockSpec((ts, w), lambda i: (jnp.minimum(i, n - 1), 0))
    prev = lambda w: pl.BlockSpec((ts, w), lambda i: (jnp.maximum(i - 1, 0), 0))

    def rows_spec(w):
        rows, cols = w.shape
        blk = next(r for r in range(16, rows + 1, 16) if rows % r == 0 and r * (n + 1) >= rows)
        return pl.BlockSpec((blk, cols), lambda i: (jnp.minimum(i, rows // blk - 1), 0))

    ffn_weights = [wg, wu, wd, w_out]
    gla_scratch = [pltpu.VMEM((GLA_KEY_WIDTH, GLA_DV), F32), pltpu.VMEM((GLA_PAIRS, ts, LANES), F32),
                   pltpu.VMEM((GLA_PAIRS, LANES, LANES), F32)]
    qkvg, lr, o_f, y_b, wg_b, wu_b, wd_b, wo_b = pl.pallas_call(
        functools.partial(_fwd_sweep_kernel, n_tiles=n, tiles_per_seq=nt),
        grid=(n + 1,),
        in_specs=[cur(D), _const_spec((1, D)), _const_spec((D, QKVG_WIDTH)), _const_spec((D, UVLR_WIDTH)),
                  _const_spec((LANES, GLA_KEY_WIDTH)),
                  _const_spec((1, GLA_KEY_WIDTH)), _const_spec((1, GMLP_WIDTH)), _const_spec((1, GMLP_WIDTH)),
                  _const_spec((GMLP_GROUPS, GMLP_CHUNK, GMLP_CHUNK)), _const_spec((GMLP_GROUPS, GMLP_CHUNK, 1))]
                 + [rows_spec(w) for w in ffn_weights],
        out_specs=[cur(QKVG_WIDTH), cur(LANES), prev(GLA_WIDTH), prev(GMLP_WIDTH)]
                  + [rows_spec(w) for w in ffn_weights],
        out_shape=[jax.ShapeDtypeStruct((B * S, QKVG_WIDTH), BF16), jax.ShapeDtypeStruct((B * S, LANES), BF16),
                   jax.ShapeDtypeStruct((B * S, GLA_WIDTH), F32), jax.ShapeDtypeStruct((B * S, GMLP_WIDTH), BF16)]
                  + [jax.ShapeDtypeStruct(w.shape, BF16) for w in ffn_weights],
        scratch_shapes=[pltpu.VMEM((ts, PROJ_PAD), F32), pltpu.VMEM((ts, PROJ_PAD), F32),
                        pltpu.VMEM((ts, D), BF16), pltpu.VMEM((GMLP_GROUPS, GMLP_CHUNK, ts), BF16)] + gla_scratch,
        compiler_params=_params(),
        name="fwd_sweep",
    )(x2d, n1g.reshape(1, D), w_qkvg, w_uvlr, wdf_p, bdf.reshape(1, -1), lng.reshape(1, -1), lnb.reshape(1, -1),
      wsp.astype(BF16), bsp.reshape(GMLP_GROUPS, GMLP_CHUNK, 1), *ffn_weights)

    def rblock(j):
        j = jnp.clip(j, 0, n - 1)
        return (j // nt) * nt + (nt - 1 - j % nt)

    rcur = lambda w: pl.BlockSpec((ts, w), lambda j: (rblock(j), 0))
    rprev = lambda w: pl.BlockSpec((ts, w), lambda j: (rblock(j - 1), 0))
    out = pl.pallas_call(
        functools.partial(_bwd_sweep_kernel, n_tiles=n, tiles_per_seq=nt, final_norm=final_norm),
        grid=(n + 1,),
        in_specs=[rcur(QKVG_WIDTH), rcur(LANES), rcur(GLA_WIDTH), rcur(GMLP_WIDTH), rcur(D),
                  _const_spec((LANES, GLA_KEY_WIDTH)), _const_spec((1, GLA_KEY_WIDTH)),
                  _const_spec((1, GLA_WIDTH)), _const_spec((D, D)),
                  _const_spec((1, D)), _const_spec((D, D_FF)), _const_spec((D, D_FF)), _const_spec((D_FF, D)),
                  _const_spec((1, D))],
        out_specs=rprev(D),
        out_shape=jax.ShapeDtypeStruct((B * S, D), F32),
        scratch_shapes=[pltpu.VMEM((ts, D), F32), pltpu.VMEM((ts, D), F32), pltpu.VMEM((ts, D), BF16),
                        pltpu.VMEM((2, ts, DOT_COLS), BF16), pltpu.VMEM((ts, D), F32),
                        pltpu.VMEM((ts, GLA_WIDTH), F32), pltpu.VMEM((ts, D), BF16)] + gla_scratch,
        compiler_params=_params(),
        name="bwd_sweep",
    )(qkvg, lr, o_f, y_b, x2d, wdb_p, bdb.reshape(1, -1), gng.reshape(1, -1), wo_b,
      n2g.reshape(1, D), wg_b, wu_b, wd_b, fng.reshape(1, D))
    return out.reshape(B, S, D)


def kernel(x, norm1_g, w_in, w_decay_f, b_decay_f, w_decay_b, b_decay_b, gla_norm_g, gmlp_ln_g, gmlp_ln_b,
           w_spatial, b_spatial, w_out, norm2_g, w_gate, w_up, w_down, final_norm_g):
    depth = norm1_g.shape[0]
    for l in range(depth):
        x = _layer(x, norm1_g[l], w_in[l], w_decay_f[l], b_decay_f[l], w_decay_b[l], b_decay_b[l],
                   gla_norm_g[l], gmlp_ln_g[l], gmlp_ln_b[l], w_spatial[l], b_spatial[l], w_out[l],
                   norm2_g[l], w_gate[l], w_up[l], w_down[l], final_norm_g, final_norm=(l == depth - 1))
    return x
```

```python
import functools

import jax
import jax.numpy as jnp
from jax import lax
from jax.experimental import pallas as pl
from jax.experimental.pallas import tpu as pltpu

F32 = jnp.float32
BF16 = jnp.bfloat16

D_MODEL = 1024
GLA_HEADS = 4
GLA_PAIRS = GLA_HEADS // 2
GLA_DK = 64
GLA_DV = 128
GLA_KEY_WIDTH = GLA_HEADS * GLA_DK
GLA_WIDTH = GLA_HEADS * GLA_DV
GLA_LOWRANK = 16
GLA_TAU = 16.0
GLA_CHUNK = 64
GMLP_GROUPS = 4
GMLP_GROUP_DIM = 128
GMLP_WIDTH = GMLP_GROUPS * GMLP_GROUP_DIM
GMLP_CHUNK = 128
D_FF = 2816
EPS = 1e-6

LANES = 128
MXU_TILE = 256
DOT_COLS = 2 * MXU_TILE
COL_Q = 0
COL_K = COL_Q + GLA_KEY_WIDTH
COL_V = COL_K + GLA_KEY_WIDTH
COL_G = COL_V + GLA_WIDTH
COL_UV = COL_G + GLA_WIDTH
COL_LR = COL_UV + 2 * GMLP_WIDTH
PROJ_PAD = COL_LR + LANES
QKVG_WIDTH = COL_UV
UVLR_WIDTH = PROJ_PAD - COL_UV

SEQ_TILE = 512
HALF_TILE = SEQ_TILE // 2
N_CHUNKS = SEQ_TILE // GLA_CHUNK
N_GMLP = SEQ_TILE // GMLP_CHUNK
FF_BOUNDS = list(range(0, D_FF, DOT_COLS)) + [D_FF]
VMEM_LIMIT_BYTES = 60 * 1024 * 1024

_TN = (((0,), (0,)), ((), ()))


def _dot(a, b):
    return jnp.dot(a, b, preferred_element_type=F32)


def _rms(x, g):
    return x * lax.rsqrt(jnp.mean(x * x, axis=-1, keepdims=True) + EPS) * g


def _log_sigmoid(z):
    return jnp.minimum(z, 0.0) - jnp.log(1.0 + jnp.exp(-jnp.abs(z)))


def _sigmoid(z):
    return 0.5 * (1.0 + jnp.tanh(0.5 * z))


def _gelu(x):
    return 0.5 * x * (1.0 + lax.erf(x * (0.5 ** 0.5)))


def _interleave(*streams, windows=None):
    windows = windows or [(0.0, 1.0)] * len(streams)
    totals = [float(sum(c for c, _ in s)) or 1.0 for s in streams]
    pos = [0] * len(streams)
    done = [0.0] * len(streams)

    def when(k):
        lo, hi = windows[k]
        return lo + (hi - lo) * (done[k] + 0.5 * streams[k][pos[k]][0]) / totals[k]

    while True:
        live = [k for k in range(len(streams)) if pos[k] < len(streams[k])]
        if not live:
            return
        k = min(live, key=when)
        cost, thunk = streams[k][pos[k]]
        thunk()
        done[k] += cost
        pos[k] += 1


def _skewed(first, second, order):
    seq = []
    for k, item in enumerate(order):
        seq.append(first(item))
        if k >= 1:
            seq.append(second(order[k - 1]))
    seq.append(second(order[-1]))
    return seq


def _gla_stages(read, lr_read, wdec_ref, bdec_ref, state_ref, b_scr, dec_scr, o_write, reverse):
    last = 0 if reverse else GLA_CHUNK - 1
    keep = {}

    def decay_dot(r0):
        def f():
            keep["z", r0] = _dot(lr_read(slice(r0, r0 + HALF_TILE)), wdec_ref[...])
        return (150, f)

    def decay_cumsum(r0):
        def f():
            la = _log_sigmoid(keep.pop(("z", r0)) + bdec_ref[...]) * (1.0 / GLA_TAU)
            r = lax.broadcasted_iota(jnp.int32, (HALF_TILE, HALF_TILE), 0)
            c = lax.broadcasted_iota(jnp.int32, (HALF_TILE, HALF_TILE), 1)
            same_chunk = (r // GLA_CHUNK) == (c // GLA_CHUNK)
            tri = jnp.where(jnp.logical_and(same_chunk, (c >= r) if reverse else (c <= r)), 1.0, 0.0).astype(BF16)
            hi = la.astype(BF16)
            lo = (la - hi.astype(F32)).astype(BF16)
            both = _dot(tri, jnp.concatenate([hi, lo], axis=1))
            b = both[:, :GLA_KEY_WIDTH] + both[:, GLA_KEY_WIDTH:]
            for p in range(GLA_PAIRS):
                b_scr[p, r0:r0 + HALF_TILE, :] = b[:, p * LANES:(p + 1) * LANES]
        return (400, f)

    def chunk_decays():
        pad = jnp.zeros((LANES - N_CHUNKS, LANES), F32)
        for p in range(GLA_PAIRS):
            b_last = b_scr[p, pl.ds(last, N_CHUNKS, stride=GLA_CHUNK), :]
            dec_scr[p] = jnp.exp(jnp.concatenate([b_last, pad], axis=0).T)

    halves = [HALF_TILE, 0] if reverse else [0, HALF_TILE]
    decay_stages = [decay_dot(r0) for r0 in halves] + [decay_cumsum(r0) for r0 in halves] + [(100, chunk_decays)]

    def first(c):
        def f():
            ci = lax.broadcasted_iota(jnp.int32, (GLA_CHUNK, LANES), 1)
            first_head = ci < GLA_DK
            rows = slice(c * GLA_CHUNK, (c + 1) * GLA_CHUNK)
            b = jnp.concatenate([b_scr[p, rows, :] for p in range(GLA_PAIRS)], axis=1)
            e_last = jnp.exp(b[last:last + 1, :])
            qd = (read(rows, slice(COL_Q, COL_K)) * (jnp.exp(b) * GLA_DK ** -0.5)).astype(BF16)
            kd = read(rows, slice(COL_K, COL_V)) * jnp.exp(-b)
            kte = (kd * e_last).astype(BF16)
            vb = read(rows, slice(COL_V, COL_G)).astype(BF16)
            scores = []
            for p in range(GLA_PAIRS):
                kd_p = kd[:, p * LANES:(p + 1) * LANES]
                kd_bd = jnp.concatenate([jnp.where(first_head, kd_p, 0.0), jnp.where(first_head, 0.0, kd_p)], axis=0)
                scores.append(_dot(qd[:, p * LANES:(p + 1) * LANES], kd_bd.T.astype(BF16)))
            scores = jnp.concatenate(scores, axis=1)
            d_state = [lax.dot_general(kte[:, p * LANES:(p + 1) * LANES],
                                       vb[:, p * 2 * GLA_DV:(p + 1) * 2 * GLA_DV], _TN,
                                       preferred_element_type=F32) for p in range(GLA_PAIRS)]
            keep[c] = (qd, vb, scores, d_state)
        return (220, f)

    def second(c):
        def f():
            ri = lax.broadcasted_iota(jnp.int32, (GLA_CHUNK, GLA_KEY_WIDTH), 0)
            ti = lax.broadcasted_iota(jnp.int32, (GLA_CHUNK, GLA_KEY_WIDTH), 1) & (GLA_DK - 1)
            causal = (ti >= ri) if reverse else (ti <= ri)
            zeros_v = jnp.zeros((GLA_CHUNK, GLA_DV), BF16)
            qd, vb, scores, d_state = keep.pop(c)
            a = jnp.where(causal, scores, 0.0).astype(BF16)
            for p in range(GLA_PAIRS):
                h0, h1 = 2 * p, 2 * p + 1
                lanes = slice(p * LANES, (p + 1) * LANES)
                s0 = state_ref[h0 * GLA_DK:(h0 + 1) * GLA_DK, :]
                s1 = state_ref[h1 * GLA_DK:(h1 + 1) * GLA_DK, :]
                rhs = jnp.concatenate([
                    jnp.concatenate([vb[:, h0 * GLA_DV:(h0 + 1) * GLA_DV], zeros_v], axis=1),
                    jnp.concatenate([zeros_v, vb[:, h1 * GLA_DV:(h1 + 1) * GLA_DV]], axis=1),
                    jnp.concatenate([s0.astype(BF16), zeros_v], axis=1),
                    jnp.concatenate([zeros_v, s1.astype(BF16)], axis=1)], axis=0)
                o_write(c, p, _dot(jnp.concatenate([a[:, lanes], qd[:, lanes]], axis=1), rhs))
                dec0 = dec_scr[p, :GLA_DK, c:c + 1]
                dec1 = dec_scr[p, GLA_DK:, c:c + 1]
                state_ref[h0 * GLA_DK:(h0 + 1) * GLA_DK, :] = (
                    s0 * dec0 + d_state[p][:GLA_DK, :GLA_DV])
                state_ref[h1 * GLA_DK:(h1 + 1) * GLA_DK, :] = (
                    s1 * dec1 + d_state[p][GLA_DK:, GLA_DV:])
        return (160, f)

    return decay_stages, first, second


def _fwd_sweep_kernel(x_ref, n1g_ref, wqkvg_ref, wuvlr_ref, wdec_ref, bdec_ref, lng_ref, lnb_ref, wsp_ref, bsp_ref,
                      wg32_ref, wu32_ref, wd32_ref, wo32_ref,
                      qkvg_ref, lr_ref, of_ref, yb_ref, wg16_ref, wu16_ref, wd16_ref, wo16_ref,
                      p_even, p_odd, h_scr, vn_scr, state_ref, b_scr, dec_scr, *, n_tiles, tiles_per_seq):
    i = pl.program_id(0)

    def cast_stages():
        def cast(src, dst):
            def f():
                dst[...] = src[...].astype(BF16)
            return (40, f)

        return [cast(wg32_ref, wg16_ref), cast(wu32_ref, wu16_ref), cast(wd32_ref, wd16_ref),
                cast(wo32_ref, wo16_ref)]

    def project_stages(p_w):
        def norm(r0):
            def f():
                h_scr[r0:r0 + HALF_TILE, :] = _rms(x_ref[r0:r0 + HALF_TILE, :], n1g_ref[...]).astype(BF16)
            return (300, f)

        def piece(r0, w_ref, base, lo, hi):
            def f():
                rows = slice(r0, r0 + HALF_TILE)
                val = _dot(h_scr[rows, :], w_ref[:, lo:hi])
                p_w[rows, base + lo:base + hi] = val
                if w_ref is wqkvg_ref:
                    qkvg_ref[rows, lo:hi] = val.astype(BF16)
                elif base + hi == PROJ_PAD:
                    lr_ref[rows, :] = val[:, COL_LR - base - lo:].astype(BF16)
            return (hi - lo, f)

        pieces = [(wqkvg_ref, 0, lo, lo + DOT_COLS) for lo in range(0, QKVG_WIDTH, DOT_COLS)]
        pieces += [(wuvlr_ref, COL_UV, 0, DOT_COLS), (wuvlr_ref, COL_UV, DOT_COLS, UVLR_WIDTH)]
        lower = [piece(0, *pc) for pc in pieces]
        upper = [piece(HALF_TILE, *pc) for pc in pieces]
        return [norm(0), lower[0], norm(HALF_TILE)] + lower[1:] + upper

    def mix_stages(p_r):
        def o_write(c, pair, val):
            of_ref[c * GLA_CHUNK:(c + 1) * GLA_CHUNK, pair * 2 * GLA_DV:(pair + 1) * 2 * GLA_DV] = val

        decay_stages, first, second = _gla_stages(
            lambda rows, cols: p_r[rows, cols], lambda rows: p_r[rows, COL_LR:PROJ_PAD].astype(BF16),
            wdec_ref, bdec_ref, state_ref, b_scr, dec_scr, o_write, reverse=False)

        keep = {}

        def gmlp_norm(n):
            def f():
                rows = slice(n * GMLP_CHUNK, (n + 1) * GMLP_CHUNK)
                vg = _gelu(p_r[rows, COL_UV + GMLP_WIDTH:COL_LR])
                mu = jnp.mean(vg, axis=-1, keepdims=True)
                vc = vg - mu
                vn = vc * lax.rsqrt(jnp.mean(vc * vc, axis=-1, keepdims=True) + EPS) * lng_ref[...] + lnb_ref[...]
                for g in range(GMLP_GROUPS):
                    vn_scr[g, :, n * GMLP_CHUNK:(n + 1) * GMLP_CHUNK] = (
                        vn[:, g * GMLP_GROUP_DIM:(g + 1) * GMLP_GROUP_DIM].astype(BF16))
            return (300, f)

        def gmlp_mix():
            for g in range(GMLP_GROUPS):
                keep[g] = _dot(wsp_ref[g], vn_scr[g]) + bsp_ref[g]

        def gmlp_gate(n):
            def f():
                rows = slice(n * GMLP_CHUNK, (n + 1) * GMLP_CHUNK)
                for g in range(GMLP_GROUPS):
                    cols = slice(g * GMLP_GROUP_DIM, (g + 1) * GMLP_GROUP_DIM)
                    u = _gelu(p_r[rows, COL_UV + g * GMLP_GROUP_DIM:COL_UV + (g + 1) * GMLP_GROUP_DIM])
                    yb_ref[rows, cols] = (u * keep[g][:, n * GMLP_CHUNK:(n + 1) * GMLP_CHUNK]).astype(BF16)
            return (200, f)

        chunks = _skewed(first, second, list(range(N_CHUNKS)))
        norms = [gmlp_norm(n) for n in range(N_GMLP)]
        gates = [gmlp_gate(n) for n in range(N_GMLP)]
        stages = list(decay_stages)
        per = (len(chunks) - N_GMLP) // N_GMLP
        for n in range(N_GMLP):
            stages += chunks[n * per:(n + 1) * per] + [norms[n]]
        stages.append((300, gmlp_mix))
        for n, st in enumerate(chunks[N_GMLP * per:]):
            stages += [st, gates[n]]
        return stages

    @pl.when(lax.rem(i - 1, tiles_per_seq) == 0)
    def _():
        state_ref[...] = jnp.zeros_like(state_ref)

    steady = jnp.logical_and(i > 0, i < n_tiles)
    windows = [(0.0, 1.0), (0.0, 0.90), (0.2, 0.8)]

    @pl.when(i == 0)
    def _():
        _interleave(project_stages(p_even), cast_stages())

    @pl.when(jnp.logical_and(steady, lax.rem(i, 2) == 0))
    def _():
        _interleave(project_stages(p_even), mix_stages(p_odd), cast_stages(), windows=windows)

    @pl.when(jnp.logical_and(steady, lax.rem(i, 2) == 1))
    def _():
        _interleave(project_stages(p_odd), mix_stages(p_even), cast_stages(), windows=windows)

    @pl.when(i == n_tiles)
    def _():
        _interleave(mix_stages(p_even if (n_tiles - 1) % 2 == 0 else p_odd), cast_stages())


def _bwd_sweep_kernel(qkvg_ref, lr_ref, of_ref, yb_ref, x_ref, wdec_ref, bdec_ref, gng_ref, wout_ref,
                      n2g_ref, wg_ref, wu_ref, wd_ref, fng_ref,
                      out_ref,
                      x2_even, x2_odd, h2_scr, a_scr, acc_scr, o_scr, y_scr, state_ref, b_scr, dec_scr,
                      *, n_tiles, tiles_per_seq, final_norm):
    j = pl.program_id(0)

    def mix_stages(x2_w):
        def o_write(c, pair, val):
            o_scr[c * GLA_CHUNK:(c + 1) * GLA_CHUNK, pair * 2 * GLA_DV:(pair + 1) * 2 * GLA_DV] = val

        decay_stages, first, second = _gla_stages(
            lambda rows, cols: qkvg_ref[rows, cols].astype(F32), lambda rows: lr_ref[rows, :],
            wdec_ref, bdec_ref, state_ref, b_scr, dec_scr, o_write, reverse=True)

        def gate(r0):
            def f():
                rows = slice(r0, r0 + HALF_TILE)
                g = qkvg_ref[rows, COL_G:COL_UV].astype(F32)
                g = g * _sigmoid(g)
                for hd in range(GLA_HEADS):
                    cols = slice(hd * GLA_DV, (hd + 1) * GLA_DV)
                    o = of_ref[rows, cols] + o_scr[rows, cols]
                    o = o * lax.rsqrt(jnp.mean(o * o, axis=-1, keepdims=True) + EPS)
                    y_scr[rows, cols] = (o * gng_ref[:, cols] * g[:, cols]).astype(BF16)
                y_scr[rows, GLA_WIDTH:] = yb_ref[rows, :]
            return (250, f)

        def out_proj(r0, lo):
            def f():
                rows = slice(r0, r0 + HALF_TILE)
                x2_w[rows, lo:lo + DOT_COLS] = (x_ref[rows, lo:lo + DOT_COLS]
                                                + _dot(y_scr[rows, :], wout_ref[:, lo:lo + DOT_COLS]))
            return (512, f)

        chunks = _skewed(first, second, list(range(N_CHUNKS - 1, -1, -1)))
        per_half = len(chunks) // 2
        upper = [gate(HALF_TILE)] + [out_proj(HALF_TILE, lo) for lo in range(0, D_MODEL, DOT_COLS)]
        lower = [gate(0)] + [out_proj(0, lo) for lo in range(0, D_MODEL, DOT_COLS)]
        tail = list(chunks[per_half + 1:])
        stages = chunks[:per_half + 1]
        for k, st in enumerate(upper):
            stages.append(st)
            stages += tail[2 * k:2 * k + 2]
        stages += tail[2 * len(upper):]
        return decay_stages, stages + lower

    def ffn_stages(x2_r):
        def norm(r0):
            def f():
                h2_scr[r0:r0 + HALF_TILE, :] = _rms(x2_r[r0:r0 + HALF_TILE, :], n2g_ref[...]).astype(BF16)
            return (300, f)

        def up(item):
            r0, c = item
            lo, hi = FF_BOUNDS[c], FF_BOUNDS[c + 1]

            def f():
                rows = slice(r0, r0 + HALF_TILE)
                h2 = h2_scr[rows, :]
                g = _dot(h2, wg_ref[:, lo:hi])
                u = _dot(h2, wu_ref[:, lo:hi])
                a_scr[c % 2, rows, :hi - lo] = (g * _sigmoid(g) * u).astype(BF16)
            return (2 * (hi - lo), f)

        def down(item):
            r0, c = item
            lo, hi = FF_BOUNDS[c], FF_BOUNDS[c + 1]

            def f():
                rows = slice(r0, r0 + HALF_TILE)
                d = _dot(a_scr[c % 2, rows, :hi - lo], wd_ref[lo:hi, :])
                if c == 0:
                    acc_scr[rows, :] = d
                else:
                    acc_scr[rows, :] += d
            return (hi - lo, f)

        def final(r0):
            def f():
                rows = slice(r0, r0 + HALF_TILE)
                y = x2_r[rows, :] + acc_scr[rows, :]
                if final_norm:
                    y = _rms(y, fng_ref[...])
                out_ref[rows, :] = y
            return (300, f)

        groups = range(len(FF_BOUNDS) - 1)
        lower = _skewed(up, down, [(0, c) for c in groups])
        upper = _skewed(up, down, [(HALF_TILE, c) for c in groups])
        return ([norm(0), lower[0], norm(HALF_TILE)] + lower[1:] + upper[:2] + [final(0)] + upper[2:]
                + [final(HALF_TILE)])

    @pl.when(lax.rem(j, tiles_per_seq) == 0)
    def _():
        state_ref[...] = jnp.zeros_like(state_ref)

    steady = jnp.logical_and(j > 0, j < n_tiles)

    @pl.when(j == 0)
    def _():
        decay, rest = mix_stages(x2_even)
        _interleave(decay + rest)

    windows = [(0.0, 1.0), (-0.02, 0.04), (0.08, 0.96)]

    @pl.when(jnp.logical_and(steady, lax.rem(j, 2) == 0))
    def _():
        _interleave(ffn_stages(x2_odd), *mix_stages(x2_even), windows=windows)

    @pl.when(jnp.logical_and(steady, lax.rem(j, 2) == 1))
    def _():
        _interleave(ffn_stages(x2_even), *mix_stages(x2_odd), windows=windows)

    @pl.when(j == n_tiles)
    def _():
        _interleave(ffn_stages(x2_even if (n_tiles - 1) % 2 == 0 else x2_odd))


def _const_spec(shape):
    return pl.BlockSpec(shape, lambda *_: (0,) * len(shape), pipeline_mode=pl.Buffered(1))


def _params():
    return pltpu.CompilerParams(dimension_semantics=("arbitrary",), vmem_limit_bytes=VMEM_LIMIT_BYTES)


def _layer(x, n1g, w_in, wdf, bdf, wdb, bdb, gng, lng, lnb, wsp, bsp, w_out, n2g, wg, wu, wd, fng, final_norm):
    B, S, D = x.shape
    ts = SEQ_TILE
    assert S % ts == 0 and D == D_MODEL
    nt = S // ts
    n = B * nt
    x2d = x.reshape(B * S, D)

    lr0 = COL_UV
    w_qkvg = w_in[:, :lr0].astype(BF16)
    w_uvlr = jnp.concatenate([w_in[:, lr0 + 2 * GLA_LOWRANK:], w_in[:, lr0:lr0 + 2 * GLA_LOWRANK],
                              jnp.zeros((D, LANES - 2 * GLA_LOWRANK), w_in.dtype)], axis=1).astype(BF16)
    zpad = jnp.zeros((LANES - 2 * GLA_LOWRANK, GLA_KEY_WIDTH), F32)
    zlr = jnp.zeros((GLA_LOWRANK, GLA_KEY_WIDTH), F32)
    wdf_p = jnp.concatenate([wdf, zlr, zpad], axis=0).astype(BF16)
    wdb_p = jnp.concatenate([zlr, wdb, zpad], axis=0).astype(BF16)

    cur = lambda w: pl.BlockSpec((ts, w), lambda i: (jnp.minimum(i, n - 1), 0))
    prev = lambda w: pl.BlockSpec((ts, w), lambda i: (jnp.maximum(i - 1, 0), 0))

    def rows_spec(w):
        rows, cols = w.shape
        blk = next(r for r in range(16, rows + 1, 16) if rows % r == 0 and r * (n + 1) >= rows)
        return pl.BlockSpec((blk, cols), lambda i: (jnp.minimum(i, rows // blk - 1), 0))

    ffn_weights = [wg, wu, wd, w_out]
    gla_scratch = [pltpu.VMEM((GLA_KEY_WIDTH, GLA_DV), F32), pltpu.VMEM((GLA_PAIRS, ts, LANES), F32),
                   pltpu.VMEM((GLA_PAIRS, LANES, LANES), F32)]
    qkvg, lr, o_f, y_b, wg_b, wu_b, wd_b, wo_b = pl.pallas_call(
        functools.partial(_fwd_sweep_kernel, n_tiles=n, tiles_per_seq=nt),
        grid=(n + 1,),
        in_specs=[cur(D), _const_spec((1, D)), _const_spec((D, QKVG_WIDTH)), _const_spec((D, UVLR_WIDTH)),
                  _const_spec((LANES, GLA_KEY_WIDTH)),
                  _const_spec((1, GLA_KEY_WIDTH)), _const_spec((1, GMLP_WIDTH)), _const_spec((1, GMLP_WIDTH)),
                  _const_spec((GMLP_GROUPS, GMLP_CHUNK, GMLP_CHUNK)), _const_spec((GMLP_GROUPS, GMLP_CHUNK, 1))]
                 + [rows_spec(w) for w in ffn_weights],
        out_specs=[cur(QKVG_WIDTH), cur(LANES), prev(GLA_WIDTH), prev(GMLP_WIDTH)]
                  + [rows_spec(w) for w in ffn_weights],
        out_shape=[jax.ShapeDtypeStruct((B * S, QKVG_WIDTH), BF16), jax.ShapeDtypeStruct((B * S, LANES), BF16),
                   jax.ShapeDtypeStruct((B * S, GLA_WIDTH), F32), jax.ShapeDtypeStruct((B * S, GMLP_WIDTH), BF16)]
                  + [jax.ShapeDtypeStruct(w.shape, BF16) for w in ffn_weights],
        scratch_shapes=[pltpu.VMEM((ts, PROJ_PAD), F32), pltpu.VMEM((ts, PROJ_PAD), F32),
                        pltpu.VMEM((ts, D), BF16), pltpu.VMEM((GMLP_GROUPS, GMLP_CHUNK, ts), BF16)] + gla_scratch,
        compiler_params=_params(),
        name="fwd_sweep",
    )(x2d, n1g.reshape(1, D), w_qkvg, w_uvlr, wdf_p, bdf.reshape(1, -1), lng.reshape(1, -1), lnb.reshape(1, -1),
      wsp.astype(BF16), bsp.reshape(GMLP_GROUPS, GMLP_CHUNK, 1), *ffn_weights)

    def rblock(j):
        j = jnp.clip(j, 0, n - 1)
        return (j // nt) * nt + (nt - 1 - j % nt)

    rcur = lambda w: pl.BlockSpec((ts, w), lambda j: (rblock(j), 0))
    rprev = lambda w: pl.BlockSpec((ts, w), lambda j: (rblock(j - 1), 0))
    out = pl.pallas_call(
        functools.partial(_bwd_sweep_kernel, n_tiles=n, tiles_per_seq=nt, final_norm=final_norm),
        grid=(n + 1,),
        in_specs=[rcur(QKVG_WIDTH), rcur(LANES), rcur(GLA_WIDTH), rcur(GMLP_WIDTH), rcur(D),
                  _const_spec((LANES, GLA_KEY_WIDTH)), _const_spec((1, GLA_KEY_WIDTH)),
                  _const_spec((1, GLA_WIDTH)), _const_spec((D, D)),
                  _const_spec((1, D)), _const_spec((D, D_FF)), _const_spec((D, D_FF)), _const_spec((D_FF, D)),
                  _const_spec((1, D))],
        out_specs=rprev(D),
        out_shape=jax.ShapeDtypeStruct((B * S, D), F32),
        scratch_shapes=[pltpu.VMEM((ts, D), F32), pltpu.VMEM((ts, D), F32), pltpu.VMEM((ts, D), BF16),
                        pltpu.VMEM((2, ts, DOT_COLS), BF16), pltpu.VMEM((ts, D), F32),
                        pltpu.VMEM((ts, GLA_WIDTH), F32), pltpu.VMEM((ts, D), BF16)] + gla_scratch,
        compiler_params=_params(),
        name="bwd_sweep",
    )(qkvg, lr, o_f, y_b, x2d, wdb_p, bdb.reshape(1, -1), gng.reshape(1, -1), wo_b,
      n2g.reshape(1, D), wg_b, wu_b, wd_b, fng.reshape(1, D))
    return out.reshape(B, S, D)


def kernel(x, norm1_g, w_in, w_decay_f, b_decay_f, w_decay_b, b_decay_b, gla_norm_g, gmlp_ln_g, gmlp_ln_b,
           w_spatial, b_spatial, w_out, norm2_g, w_gate, w_up, w_down, final_norm_g):
    depth = norm1_g.shape[0]
    for l in range(depth):
        x = _layer(x, norm1_g[l], w_in[l], w_decay_f[l], b_decay_f[l], w_decay_b[l], b_decay_b[l],
                   gla_norm_g[l], gmlp_ln_g[l], gmlp_ln_b[l], w_spatial[l], b_spatial[l], w_out[l],
                   norm2_g[l], w_gate[l], w_up[l], w_down[l], final_norm_g, final_norm=(l == depth - 1))
    return x
```

```python
import functools

import jax
import jax.numpy as jnp
from jax import lax
from jax.experimental import pallas as pl
from jax.experimental.pallas import tpu as pltpu

F32 = jnp.float32
BF16 = jnp.bfloat16

D_MODEL = 1024
GLA_HEADS = 4
GLA_PAIRS = GLA_HEADS // 2
GLA_DK = 64
GLA_DV = 128
GLA_KEY_WIDTH = GLA_HEADS * GLA_DK
GLA_WIDTH = GLA_HEADS * GLA_DV
GLA_LOWRANK = 16
GLA_TAU = 16.0
GLA_CHUNK = 64
GMLP_GROUPS = 4
GMLP_GROUP_DIM = 128
GMLP_WIDTH = GMLP_GROUPS * GMLP_GROUP_DIM
GMLP_CHUNK = 128
D_FF = 2816
EPS = 1e-6

LANES = 128
MXU_TILE = 256
DOT_COLS = 2 * MXU_TILE
COL_Q = 0
COL_K = COL_Q + GLA_KEY_WIDTH
COL_V = COL_K + GLA_KEY_WIDTH
COL_G = COL_V + GLA_WIDTH
COL_UV = COL_G + GLA_WIDTH
COL_LR = COL_UV + 2 * GMLP_WIDTH
PROJ_PAD = COL_LR + LANES
QKVG_WIDTH = COL_UV
UVLR_WIDTH = PROJ_PAD - COL_UV

SEQ_TILE = 512
HALF_TILE = SEQ_TILE // 2
N_CHUNKS = SEQ_TILE // GLA_CHUNK
N_GMLP = SEQ_TILE // GMLP_CHUNK
FF_BOUNDS = list(range(0, D_FF, DOT_COLS)) + [D_FF]
VMEM_LIMIT_BYTES = 60 * 1024 * 1024

_TN = (((0,), (0,)), ((), ()))


def _dot(a, b):
    return jnp.dot(a, b, preferred_element_type=F32)


def _rms(x, g):
    return x * lax.rsqrt(jnp.mean(x * x, axis=-1, keepdims=True) + EPS) * g


def _log_sigmoid(z):
    return jnp.minimum(z, 0.0) - jnp.log(1.0 + jnp.exp(-jnp.abs(z)))


def _sigmoid(z):
    return 0.5 * (1.0 + jnp.tanh(0.5 * z))


def _gelu(x):
    return 0.5 * x * (1.0 + lax.erf(x * (0.5 ** 0.5)))


def _interleave(*streams, windows=None):
    windows = windows or [(0.0, 1.0)] * len(streams)
    totals = [float(sum(c for c, _ in s)) or 1.0 for s in streams]
    pos = [0] * len(streams)
    done = [0.0] * len(streams)

    def when(k):
        lo, hi = windows[k]
        return lo + (hi - lo) * (done[k] + 0.5 * streams[k][pos[k]][0]) / totals[k]

    while True:
        live = [k for k in range(len(streams)) if pos[k] < len(streams[k])]
        if not live:
            return
        k = min(live, key=when)
        cost, thunk = streams[k][pos[k]]
        thunk()
        done[k] += cost
        pos[k] += 1


def _skewed(first, second, order):
    seq = []
    for k, item in enumerate(order):
        seq.append(first(item))
        if k >= 1:
            seq.append(second(order[k - 1]))
    seq.append(second(order[-1]))
    return seq


def _gla_stages(read, lr_read, wdec_ref, bdec_ref, state_ref, b_scr, dec_scr, o_write, reverse):
    last = 0 if reverse else GLA_CHUNK - 1
    keep = {}

    def decay_dot(r0):
        def f():
            keep["z", r0] = _dot(lr_read(slice(r0, r0 + HALF_TILE)), wdec_ref[...])
        return (150, f)

    def decay_cumsum(r0):
        def f():
            la = _log_sigmoid(keep.pop(("z", r0)) + bdec_ref[...]) * (1.0 / GLA_TAU)
            r = lax.broadcasted_iota(jnp.int32, (HALF_TILE, HALF_TILE), 0)
            c = lax.broadcasted_iota(jnp.int32, (HALF_TILE, HALF_TILE), 1)
            same_chunk = (r // GLA_CHUNK) == (c // GLA_CHUNK)
            tri = jnp.where(jnp.logical_and(same_chunk, (c >= r) if reverse else (c <= r)), 1.0, 0.0).astype(BF16)
            hi = la.astype(BF16)
            lo = (la - hi.astype(F32)).astype(BF16)
            both = _dot(tri, jnp.concatenate([hi, lo], axis=1))
            b = both[:, :GLA_KEY_WIDTH] + both[:, GLA_KEY_WIDTH:]
            for p in range(GLA_PAIRS):
                b_scr[p, r0:r0 + HALF_TILE, :] = b[:, p * LANES:(p + 1) * LANES]
        return (400, f)

    def chunk_decays():
        pad = jnp.zeros((LANES - N_CHUNKS, LANES), F32)
        for p in range(GLA_PAIRS):
            b_last = b_scr[p, pl.ds(last, N_CHUNKS, stride=GLA_CHUNK), :]
            dec_scr[p] = jnp.exp(jnp.concatenate([b_last, pad], axis=0).T)

    halves = [HALF_TILE, 0] if reverse else [0, HALF_TILE]
    decay_stages = [decay_dot(r0) for r0 in halves] + [decay_cumsum(r0) for r0 in halves] + [(100, chunk_decays)]

    def first(c):
        def f():
            ci = lax.broadcasted_iota(jnp.int32, (GLA_CHUNK, LANES), 1)
            first_head = ci < GLA_DK
            rows = slice(c * GLA_CHUNK, (c + 1) * GLA_CHUNK)
            b = jnp.concatenate([b_scr[p, rows, :] for p in range(GLA_PAIRS)], axis=1)
            e_last = jnp.exp(b[last:last + 1, :])
            qd = (read(rows, slice(COL_Q, COL_K)) * (jnp.exp(b) * GLA_DK ** -0.5)).astype(BF16)
            kd = read(rows, slice(COL_K, COL_V)) * jnp.exp(-b)
            kte = (kd * e_last).astype(BF16)
            vb = read(rows, slice(COL_V, COL_G)).astype(BF16)
            scores = []
            for p in range(GLA_PAIRS):
                kd_p = kd[:, p * LANES:(p + 1) * LANES]
                kd_bd = jnp.concatenate([jnp.where(first_head, kd_p, 0.0), jnp.where(first_head, 0.0, kd_p)], axis=0)
                scores.append(_dot(qd[:, p * LANES:(p + 1) * LANES], kd_bd.T.astype(BF16)))
            scores = jnp.concatenate(scores, axis=1)
            d_state = [lax.dot_general(kte[:, p * LANES:(p + 1) * LANES],
                                       vb[:, p * 2 * GLA_DV:(p + 1) * 2 * GLA_DV], _TN,
                                       preferred_element_type=F32) for p in range(GLA_PAIRS)]
            keep[c] = (qd, vb, scores, d_state)
        return (220, f)

    def second(c):
        def f():
            ri = lax.broadcasted_iota(jnp.int32, (GLA_CHUNK, GLA_KEY_WIDTH), 0)
            ti = lax.broadcasted_iota(jnp.int32, (GLA_CHUNK, GLA_KEY_WIDTH), 1) & (GLA_DK - 1)
            causal = (ti >= ri) if reverse else (ti <= ri)
            zeros_v = jnp.zeros((GLA_CHUNK, GLA_DV), BF16)
            qd, vb, scores, d_state = keep.pop(c)
            a = jnp.where(causal, scores, 0.0).astype(BF16)
            for p in range(GLA_PAIRS):
                h0, h1 = 2 * p, 2 * p + 1
                lanes = slice(p * LANES, (p + 1) * LANES)
                s0 = state_ref[h0 * GLA_DK:(h0 + 1) * GLA_DK, :]
                s1 = state_ref[h1 * GLA_DK:(h1 + 1) * GLA_DK, :]
                rhs = jnp.concatenate([
                    jnp.concatenate([vb[:, h0 * GLA_DV:(h0 + 1) * GLA_DV], zeros_v], axis=1),
                    jnp.concatenate([zeros_v, vb[:, h1 * GLA_DV:(h1 + 1) * GLA_DV]], axis=1),
                    jnp.concatenate([s0.astype(BF16), zeros_v], axis=1),
                    jnp.concatenate([zeros_v, s1.astype(BF16)], axis=1)], axis=0)
                o_write(c, p, _dot(jnp.concatenate([a[:, lanes], qd[:, lanes]], axis=1), rhs))
                dec0 = dec_scr[p, :GLA_DK, c:c + 1]
                dec1 = dec_scr[p, GLA_DK:, c:c + 1]
                state_ref[h0 * GLA_DK:(h0 + 1) * GLA_DK, :] = (
                    s0 * dec0 + d_state[p][:GLA_DK, :GLA_DV])
                state_ref[h1 * GLA_DK:(h1 + 1) * GLA_DK, :] = (
                    s1 * dec1 + d_state[p][GLA_DK:, GLA_DV:])
        return (160, f)

    return decay_stages, first, second


def _fwd_sweep_kernel(x_ref, n1g_ref, wqkvg_ref, wuvlr_ref, wdec_ref, bdec_ref, lng_ref, lnb_ref, wsp_ref, bsp_ref,
                      wg32_ref, wu32_ref, wd32_ref, wo32_ref,
                      qkvg_ref, of_ref, yb_ref, wg16_ref, wu16_ref, wd16_ref, wo16_ref,
                      p_0, p_1, p_2, p_3, h_scr, vn_scr, state_ref, b_scr, dec_scr, *, n_tiles, tiles_per_seq):
    i = pl.program_id(0)
    n_steps = n_tiles // 2
    x_blk, qkvg_blk, of_blk, yb_blk = x_ref, qkvg_ref, of_ref, yb_ref

    def tile_of(ref, t):
        return ref.at[t * SEQ_TILE:(t + 1) * SEQ_TILE]

    def cast_stages():
        def cast(src, dst):
            def f():
                dst[...] = src[...].astype(BF16)
            return (40, f)

        return [cast(wg32_ref, wg16_ref), cast(wu32_ref, wu16_ref), cast(wd32_ref, wd16_ref),
                cast(wo32_ref, wo16_ref)]

    def project_stages(p_w, t):
        x_ref, qkvg_ref = tile_of(x_blk, t), tile_of(qkvg_blk, t)

        def norm(r0):
            def f():
                h_scr[r0:r0 + HALF_TILE, :] = _rms(x_ref[r0:r0 + HALF_TILE, :], n1g_ref[...]).astype(BF16)
            return (300, f)

        def piece(r0, w_ref, base, lo, hi):
            def f():
                rows = slice(r0, r0 + HALF_TILE)
                val = _dot(h_scr[rows, :], w_ref[:, lo:hi])
                p_w[rows, base + lo:base + hi] = val
                if w_ref is wqkvg_ref:
                    qkvg_ref[rows, lo:hi] = val.astype(BF16)
                elif base + hi == PROJ_PAD:
                    qkvg_ref[rows, QKVG_WIDTH:] = val[:, COL_LR - base - lo:].astype(BF16)
            return (hi - lo, f)

        pieces = [(wqkvg_ref, 0, lo, lo + DOT_COLS) for lo in range(0, QKVG_WIDTH, DOT_COLS)]
        pieces += [(wuvlr_ref, COL_UV, 0, DOT_COLS), (wuvlr_ref, COL_UV, DOT_COLS, UVLR_WIDTH)]
        lower = [piece(0, *pc) for pc in pieces]
        upper = [piece(HALF_TILE, *pc) for pc in pieces]
        return [norm(0), lower[0], norm(HALF_TILE)] + lower[1:] + upper

    def mix_stages(p_r, t):
        of_ref, yb_ref = tile_of(of_blk, t), tile_of(yb_blk, t)

        def o_write(c, pair, val):
            of_ref[c * GLA_CHUNK:(c + 1) * GLA_CHUNK, pair * 2 * GLA_DV:(pair + 1) * 2 * GLA_DV] = val

        decay_stages, first, second = _gla_stages(
            lambda rows, cols: p_r[rows, cols], lambda rows: p_r[rows, COL_LR:PROJ_PAD].astype(BF16),
            wdec_ref, bdec_ref, state_ref, b_scr, dec_scr, o_write, reverse=False)

        keep = {}

        def gmlp_norm(n):
            def f():
                rows = slice(n * GMLP_CHUNK, (n + 1) * GMLP_CHUNK)
                vg = _gelu(p_r[rows, COL_UV + GMLP_WIDTH:COL_LR])
                mu = jnp.mean(vg, axis=-1, keepdims=True)
                vc = vg - mu
                vn = vc * lax.rsqrt(jnp.mean(vc * vc, axis=-1, keepdims=True) + EPS) * lng_ref[...] + lnb_ref[...]
                for g in range(GMLP_GROUPS):
                    vn_scr[g, :, n * GMLP_CHUNK:(n + 1) * GMLP_CHUNK] = (
                        vn[:, g * GMLP_GROUP_DIM:(g + 1) * GMLP_GROUP_DIM].astype(BF16))
            return (300, f)

        def gmlp_mix():
            for g in range(GMLP_GROUPS):
                keep[g] = _dot(wsp_ref[g], vn_scr[g]) + bsp_ref[g]

        def gmlp_gate(n):
            def f():
                rows = slice(n * GMLP_CHUNK, (n + 1) * GMLP_CHUNK)
                for g in range(GMLP_GROUPS):
                    cols = slice(g * GMLP_GROUP_DIM, (g + 1) * GMLP_GROUP_DIM)
                    u = _gelu(p_r[rows, COL_UV + g * GMLP_GROUP_DIM:COL_UV + (g + 1) * GMLP_GROUP_DIM])
                    yb_ref[rows, cols] = (u * keep[g][:, n * GMLP_CHUNK:(n + 1) * GMLP_CHUNK]).astype(BF16)
            return (200, f)

        chunks = _skewed(first, second, list(range(N_CHUNKS)))
        norms = [gmlp_norm(n) for n in range(N_GMLP)]
        gates = [gmlp_gate(n) for n in range(N_GMLP)]
        stages = list(decay_stages)
        per = (len(chunks) - N_GMLP) // N_GMLP
        for n in range(N_GMLP):
            stages += chunks[n * per:(n + 1) * per] + [norms[n]]
        stages.append((300, gmlp_mix))
        for n, st in enumerate(chunks[N_GMLP * per:]):
            stages += [st, gates[n]]
        return stages

    @pl.when(lax.rem(2 * (i - 1), tiles_per_seq) == 0)
    def _():
        state_ref[...] = jnp.zeros_like(state_ref)

    steady = jnp.logical_and(i > 0, i < n_steps)
    windows = [(0.0, 1.0), (0.0, 0.90), (0.2, 0.8)]

    def project(pair):
        return project_stages(pair[0], 0) + project_stages(pair[1], 1)

    def mix(pair):
        return mix_stages(pair[0], 0) + mix_stages(pair[1], 1)

    pair_a, pair_b = (p_0, p_1), (p_2, p_3)

    @pl.when(i == 0)
    def _():
        _interleave(project(pair_a), cast_stages())

    @pl.when(jnp.logical_and(steady, lax.rem(i, 2) == 0))
    def _():
        _interleave(project(pair_a), mix(pair_b), cast_stages(), windows=windows)

    @pl.when(jnp.logical_and(steady, lax.rem(i, 2) == 1))
    def _():
        _interleave(project(pair_b), mix(pair_a), cast_stages(), windows=windows)

    @pl.when(i == n_steps)
    def _():
        _interleave(mix(pair_a if (n_steps - 1) % 2 == 0 else pair_b), cast_stages())


def _bwd_sweep_kernel(qkvg_ref, of_ref, yb_ref, x_ref, wdec_ref, bdec_ref, gng_ref, wout_ref,
                      n2g_ref, wg_ref, wu_ref, wd_ref, fng_ref,
                      out_ref,
                      x2_even, x2_odd, h2_scr, a_scr, acc_scr, o_scr, y_scr, state_ref, b_scr, dec_scr,
                      *, n_tiles, tiles_per_seq, final_norm):
    j = pl.program_id(0)

    def mix_stages(x2_w):
        def o_write(c, pair, val):
            o_scr[c * GLA_CHUNK:(c + 1) * GLA_CHUNK, pair * 2 * GLA_DV:(pair + 1) * 2 * GLA_DV] = val

        decay_stages, first, second = _gla_stages(
            lambda rows, cols: qkvg_ref[rows, cols].astype(F32), lambda rows: qkvg_ref[rows, QKVG_WIDTH:],
            wdec_ref, bdec_ref, state_ref, b_scr, dec_scr, o_write, reverse=True)

        def gate(r0):
            def f():
                rows = slice(r0, r0 + HALF_TILE)
                g = qkvg_ref[rows, COL_G:COL_UV].astype(F32)
                g = g * _sigmoid(g)
                for hd in range(GLA_HEADS):
                    cols = slice(hd * GLA_DV, (hd + 1) * GLA_DV)
                    o = of_ref[rows, cols] + o_scr[rows, cols]
                    o = o * lax.rsqrt(jnp.mean(o * o, axis=-1, keepdims=True) + EPS)
                    y_scr[rows, cols] = (o * gng_ref[:, cols] * g[:, cols]).astype(BF16)
                y_scr[rows, GLA_WIDTH:] = yb_ref[rows, :]
            return (250, f)

        def out_proj(r0, lo):
            def f():
                rows = slice(r0, r0 + HALF_TILE)
                x2_w[rows, lo:lo + DOT_COLS] = (x_ref[rows, lo:lo + DOT_COLS]
                                                + _dot(y_scr[rows, :], wout_ref[:, lo:lo + DOT_COLS]))
            return (512, f)

        chunks = _skewed(first, second, list(range(N_CHUNKS - 1, -1, -1)))
        per_half = len(chunks) // 2
        upper = [gate(HALF_TILE)] + [out_proj(HALF_TILE, lo) for lo in range(0, D_MODEL, DOT_COLS)]
        lower = [gate(0)] + [out_proj(0, lo) for lo in range(0, D_MODEL, DOT_COLS)]
        tail = list(chunks[per_half + 1:])
        stages = chunks[:per_half + 1]
        for k, st in enumerate(upper):
            stages.append(st)
            stages += tail[2 * k:2 * k + 2]
        stages += tail[2 * len(upper):]
        return decay_stages, stages + lower

    def ffn_stages(x2_r):
        def norm(r0):
            def f():
                h2_scr[r0:r0 + HALF_TILE, :] = _rms(x2_r[r0:r0 + HALF_TILE, :], n2g_ref[...]).astype(BF16)
            return (300, f)

        def up(item):
            r0, c = item
            lo, hi = FF_BOUNDS[c], FF_BOUNDS[c + 1]

            def f():
                rows = slice(r0, r0 + HALF_TILE)
                h2 = h2_scr[rows, :]
                g = _dot(h2, wg_ref[:, lo:hi])
                u = _dot(h2, wu_ref[:, lo:hi])
                a_scr[c % 2, rows, :hi - lo] = (g * _sigmoid(g) * u).astype(BF16)
            return (2 * (hi - lo), f)

        def down(item):
            r0, c = item
            lo, hi = FF_BOUNDS[c], FF_BOUNDS[c + 1]

            def f():
                rows = slice(r0, r0 + HALF_TILE)
                d = _dot(a_scr[c % 2, rows, :hi - lo], wd_ref[lo:hi, :])
                if c == 0:
                    acc_scr[rows, :] = d
                else:
                    acc_scr[rows, :] += d
            return (hi - lo, f)

        def final(r0):
            def f():
                rows = slice(r0, r0 + HALF_TILE)
                y = x2_r[rows, :] + acc_scr[rows, :]
                if final_norm:
                    y = _rms(y, fng_ref[...])
                out_ref[rows, :] = y
            return (300, f)

        groups = range(len(FF_BOUNDS) - 1)
        lower = _skewed(up, down, [(0, c) for c in groups])
        upper = _skewed(up, down, [(HALF_TILE, c) for c in groups])
        return ([norm(0), lower[0], norm(HALF_TILE)] + lower[1:] + upper[:2] + [final(0)] + upper[2:]
                + [final(HALF_TILE)])

    @pl.when(lax.rem(j, tiles_per_seq) == 0)
    def _():
        state_ref[...] = jnp.zeros_like(state_ref)

    steady = jnp.logical_and(j > 0, j < n_tiles)

    @pl.when(j == 0)
    def _():
        decay, rest = mix_stages(x2_even)
        _interleave(decay + rest)

    windows = [(0.0, 1.0), (-0.02, 0.04), (0.08, 0.96)]

    @pl.when(jnp.logical_and(steady, lax.rem(j, 2) == 0))
    def _():
        _interleave(ffn_stages(x2_odd), *mix_stages(x2_even), windows=windows)

    @pl.when(jnp.logical_and(steady, lax.rem(j, 2) == 1))
    def _():
        _interleave(ffn_stages(x2_even), *mix_stages(x2_odd), windows=windows)

    @pl.when(j == n_tiles)
    def _():
        _interleave(ffn_stages(x2_even if (n_tiles - 1) % 2 == 0 else x2_odd))


def _const_spec(shape):
    return pl.BlockSpec(shape, lambda *_: (0,) * len(shape), pipeline_mode=pl.Buffered(1))


def _params():
    return pltpu.CompilerParams(dimension_semantics=("arbitrary",), vmem_limit_bytes=VMEM_LIMIT_BYTES)


def _layer(x, n1g, w_in, wdf, bdf, wdb, bdb, gng, lng, lnb, wsp, bsp, w_out, n2g, wg, wu, wd, fng, final_norm):
    B, S, D = x.shape
    ts = SEQ_TILE
    assert S % ts == 0 and D == D_MODEL
    nt = S // ts
    n = B * nt
    x2d = x.reshape(B * S, D)

    lr0 = COL_UV
    w_qkvg = w_in[:, :lr0].astype(BF16)
    w_uvlr = jnp.concatenate([w_in[:, lr0 + 2 * GLA_LOWRANK:], w_in[:, lr0:lr0 + 2 * GLA_LOWRANK],
                              jnp.zeros((D, LANES - 2 * GLA_LOWRANK), w_in.dtype)], axis=1).astype(BF16)
    zpad = jnp.zeros((LANES - 2 * GLA_LOWRANK, GLA_KEY_WIDTH), F32)
    zlr = jnp.zeros((GLA_LOWRANK, GLA_KEY_WIDTH), F32)
    wdf_p = jnp.concatenate([wdf, zlr, zpad], axis=0).astype(BF16)
    wdb_p = jnp.concatenate([zlr, wdb, zpad], axis=0).astype(BF16)

    assert n % 2 == 0 and nt % 2 == 0
    n2 = n // 2
    cur = lambda w: pl.BlockSpec((2 * ts, w), lambda i: (jnp.minimum(i, n2 - 1), 0))
    prev = lambda w: pl.BlockSpec((2 * ts, w), lambda i: (jnp.maximum(i - 1, 0), 0))

    def rows_spec(w):
        rows, cols = w.shape
        blk = next(r for r in range(16, rows + 1, 16) if rows % r == 0 and r * (n2 + 1) >= rows)
        return pl.BlockSpec((blk, cols), lambda i: (jnp.minimum(i, rows // blk - 1), 0))

    ffn_weights = [wg, wu, wd, w_out]
    gla_scratch = [pltpu.VMEM((GLA_KEY_WIDTH, GLA_DV), F32), pltpu.VMEM((GLA_PAIRS, ts, LANES), F32),
                   pltpu.VMEM((GLA_PAIRS, LANES, LANES), F32)]
    qkvg, o_f, y_b, wg_b, wu_b, wd_b, wo_b = pl.pallas_call(
        functools.partial(_fwd_sweep_kernel, n_tiles=n, tiles_per_seq=nt),
        grid=(n2 + 1,),
        in_specs=[cur(D), _const_spec((1, D)), _const_spec((D, QKVG_WIDTH)), _const_spec((D, UVLR_WIDTH)),
                  _const_spec((LANES, GLA_KEY_WIDTH)),
                  _const_spec((1, GLA_KEY_WIDTH)), _const_spec((1, GMLP_WIDTH)), _const_spec((1, GMLP_WIDTH)),
                  _const_spec((GMLP_GROUPS, GMLP_CHUNK, GMLP_CHUNK)), _const_spec((GMLP_GROUPS, GMLP_CHUNK, 1))]
                 + [rows_spec(w) for w in ffn_weights],
        out_specs=[cur(QKVG_WIDTH + LANES), prev(GLA_WIDTH), prev(GMLP_WIDTH)]
                  + [rows_spec(w) for w in ffn_weights],
        out_shape=[jax.ShapeDtypeStruct((B * S, QKVG_WIDTH + LANES), BF16),
                   jax.ShapeDtypeStruct((B * S, GLA_WIDTH), F32), jax.ShapeDtypeStruct((B * S, GMLP_WIDTH), BF16)]
                  + [jax.ShapeDtypeStruct(w.shape, BF16) for w in ffn_weights],
        scratch_shapes=[pltpu.VMEM((ts, PROJ_PAD), F32)] * 4
                       + [pltpu.VMEM((ts, D), BF16), pltpu.VMEM((GMLP_GROUPS, GMLP_CHUNK, ts), BF16)] + gla_scratch,
        compiler_params=_params(),
        name="fwd_sweep",
    )(x2d, n1g.reshape(1, D), w_qkvg, w_uvlr, wdf_p, bdf.reshape(1, -1), lng.reshape(1, -1), lnb.reshape(1, -1),
      wsp.astype(BF16), bsp.reshape(GMLP_GROUPS, GMLP_CHUNK, 1), *ffn_weights)

    def rblock(j):
        j = jnp.clip(j, 0, n - 1)
        return (j // nt) * nt + (nt - 1 - j % nt)

    rcur = lambda w: pl.BlockSpec((ts, w), lambda j: (rblock(j), 0))
    rprev = lambda w: pl.BlockSpec((ts, w), lambda j: (rblock(j - 1), 0))
    out = pl.pallas_call(
        functools.partial(_bwd_sweep_kernel, n_tiles=n, tiles_per_seq=nt, final_norm=final_norm),
        grid=(n + 1,),
        in_specs=[rcur(QKVG_WIDTH + LANES), rcur(GLA_WIDTH), rcur(GMLP_WIDTH), rcur(D),
                  _const_spec((LANES, GLA_KEY_WIDTH)), _const_spec((1, GLA_KEY_WIDTH)),
                  _const_spec((1, GLA_WIDTH)), _const_spec((D, D)),
                  _const_spec((1, D)), _const_spec((D, D_FF)), _const_spec((D, D_FF)), _const_spec((D_FF, D)),
                  _const_spec((1, D))],
        out_specs=rprev(D),
        out_shape=jax.ShapeDtypeStruct((B * S, D), F32),
        scratch_shapes=[pltpu.VMEM((ts, D), F32), pltpu.VMEM((ts, D), F32), pltpu.VMEM((ts, D), BF16),
                        pltpu.VMEM((2, ts, DOT_COLS), BF16), pltpu.VMEM((ts, D), F32),
                        pltpu.VMEM((ts, GLA_WIDTH), F32), pltpu.VMEM((ts, D), BF16)] + gla_scratch,
        compiler_params=_params(),
        name="bwd_sweep",
    )(qkvg, o_f, y_b, x2d, wdb_p, bdb.reshape(1, -1), gng.reshape(1, -1), wo_b,
      n2g.reshape(1, D), wg_b, wu_b, wd_b, fng.reshape(1, D))
    return out.reshape(B, S, D)


def kernel(x, norm1_g, w_in, w_decay_f, b_decay_f, w_decay_b, b_decay_b, gla_norm_g, gmlp_ln_g, gmlp_ln_b,
           w_spatial, b_spatial, w_out, norm2_g, w_gate, w_up, w_down, final_norm_g):
    depth = norm1_g.shape[0]
    for l in range(depth):
        x = _layer(x, norm1_g[l], w_in[l], w_decay_f[l], b_decay_f[l], w_decay_b[l], b_decay_b[l],
                   gla_norm_g[l], gmlp_ln_g[l], gmlp_ln_b[l], w_spatial[l], b_spatial[l], w_out[l],
                   norm2_g[l], w_gate[l], w_up[l], w_down[l], final_norm_g, final_norm=(l == depth - 1))
    return x
```
